```python
import jax, jax.numpy as jnp
from jax import lax
import numpy as np

D_MODEL = 1024
BATCH = 16
SEQ = 4096
DEPTH = 4

N_MIXERS = 2
N_A_LAYERS = (DEPTH + 1) // 2
N_B_LAYERS = DEPTH // 2

ML_HEADS = 8
ML_DV = D_MODEL // ML_HEADS
ML_DQK = ML_DV // 2
ML_CHUNK = 64
ML_QK_W = ML_HEADS * ML_DQK
ML_V_W = ML_HEADS * ML_DV
ML_PROJ = 2 * ML_QK_W + 2 * ML_V_W + 2 * ML_HEADS

LRU_WIDTH = D_MODEL
LRU_HEADS = 4
LRU_BW = LRU_WIDTH // LRU_HEADS
CONV_W = 4
LRU_C = 8.0

D_FF = 3584
N_EXPERTS = 8
TOP_K = 2
RMS_EPS = 1e-6

kernel_name = 'hybrid_mlstm_rglru_moe_adaln'


def rms_norm(x, g):
    x32 = x.astype(jnp.float32)
    y = x32 * lax.rsqrt(jnp.mean(jnp.square(x32), axis=-1, keepdims=True) + RMS_EPS)
    return (y * g.astype(jnp.float32)).astype(x.dtype)


def modulate(h, shift, scale):
    return h * (1 + scale[:, None, :]) + shift[:, None, :]


def swiglu(h, w13, w2):
    a, b = jnp.split(h @ w13, 2, axis=-1)
    return (jax.nn.silu(a) * b) @ w2


def mlstm_chunk_step(carry, xs):
    C, n, m = carry
    q, k, v, i_pre, log_f = xs
    L = q.shape[2]
    b = jnp.cumsum(log_f, axis=-1)
    g = b[..., -1]
    causal = jnp.tril(jnp.ones((L, L), dtype=bool))
    d = jnp.where(causal, b[..., :, None] - b[..., None, :] + i_pre[..., None, :], -jnp.inf)
    inter = b + m[..., None]
    m_t = jnp.maximum(inter, jnp.max(d, axis=-1))
    w = jnp.exp(d - m_t[..., None])
    s = jnp.einsum('bhtd,bhsd->bhts', q, k) * w
    e_inter = jnp.exp(inter - m_t)
    num = jnp.einsum('bhts,bhsv->bhtv', s, v) + e_inter[..., None] * jnp.einsum('bhtd,bhdv->bhtv', q, C)
    den = jnp.sum(s, axis=-1) + e_inter * jnp.einsum('bhtd,bhd->bht', q, n)
    h = num / jnp.maximum(jnp.abs(den), jnp.exp(-m_t))[..., None]
    log_w_s = g[..., None] - b + i_pre
    m_new = jnp.maximum(g + m, jnp.max(log_w_s, axis=-1))
    w_s = jnp.exp(log_w_s - m_new[..., None])
    decay = jnp.exp(g + m - m_new)
    C_new = decay[..., None, None] * C + jnp.einsum('bhsd,bhsv->bhdv', k * w_s[..., None], v)
    n_new = decay[..., None] * n + jnp.einsum('bhs,bhsd->bhd', w_s, k)
    return (C_new, n_new, m_new), h


def mlstm_mixer(h, w_in, gate_b, head_g, w_out):
    B, S, _ = h.shape
    nc = S // ML_CHUNK
    f32 = jnp.float32
    q, k, v, o, gts = jnp.split(h @ w_in, [ML_QK_W, 2 * ML_QK_W, 2 * ML_QK_W + ML_V_W, 2 * ML_QK_W + 2 * ML_V_W], axis=-1)
    gts = gts.astype(f32) + gate_b.astype(f32)
    i_pre, f_pre = jnp.split(gts, 2, axis=-1)

    def to_chunks(t, dim):
        return t.astype(f32).reshape(B, nc, ML_CHUNK, ML_HEADS, dim).transpose(1, 0, 3, 2, 4)

    def gate_chunks(t):
        return t.reshape(B, nc, ML_CHUNK, ML_HEADS).transpose(1, 0, 3, 2)

    xs = (to_chunks(q, ML_DQK) * (ML_DQK ** -0.5), to_chunks(k, ML_DQK), to_chunks(v, ML_DV),
          gate_chunks(i_pre), gate_chunks(jax.nn.log_sigmoid(f_pre)))
    init = (jnp.zeros((B, ML_HEADS, ML_DQK, ML_DV), f32),
            jnp.zeros((B, ML_HEADS, ML_DQK), f32),
            jnp.zeros((B, ML_HEADS), f32))
    _, hs = lax.scan(mlstm_chunk_step, init, xs)
    hs = hs.transpose(1, 0, 3, 2, 4).reshape(B, S, ML_HEADS, ML_DV)
    hs = rms_norm(hs, head_g.reshape(ML_HEADS, ML_DV))
    hs = hs.reshape(B, S, ML_V_W) * jax.nn.sigmoid(o.astype(f32))
    return hs.astype(h.dtype) @ w_out


def lru_combine(left, right):
    a_l, b_l = left
    a_r, b_r = right
    return (a_l * a_r, a_r * b_l + b_r)


def rglru_mixer(h, w_in, conv_w, conv_b, gate_w, gate_b, a_param, w_out):
    B, S, _ = h.shape
    f32 = jnp.float32
    gate_branch, xr = jnp.split(h @ w_in, 2, axis=-1)
    xpad = jnp.pad(xr, ((0, 0), (CONV_W - 1, 0), (0, 0)))
    xc = conv_b + xpad[:, 0:S, :] * conv_w[0]
    for j in range(1, CONV_W):
        xc = xc + xpad[:, j:j + S, :] * conv_w[j]
    xb = xc.reshape(B, S, LRU_HEADS, LRU_BW)
    gts = jnp.einsum('bshi,hio->bsho', xb, gate_w) + gate_b
    gx, ga = jnp.split(gts.astype(f32), 2, axis=-1)
    gx = jax.nn.sigmoid(gx.reshape(B, S, LRU_WIDTH))
    ga = jax.nn.sigmoid(ga.reshape(B, S, LRU_WIDTH))
    log_a = -LRU_C * ga * jax.nn.softplus(-a_param.astype(f32))
    a = jnp.exp(log_a)
    u = xc.astype(f32) * gx * jnp.sqrt(-jnp.expm1(2.0 * log_a))
    _, hseq = lax.associative_scan(lru_combine, (a, u), axis=1)
    y = hseq * jax.nn.gelu(gate_branch.astype(f32))
    return y.astype(h.dtype) @ w_out


def moe_swiglu(h, router_w, router_b, w13, w2):
    logits = (h @ router_w).astype(jnp.float32) + router_b.astype(jnp.float32)
    top_v, top_i = lax.top_k(logits, TOP_K)
    top_w = jax.nn.softmax(top_v, axis=-1)
    combine = jnp.einsum('bsk,bske->bse', top_w, jax.nn.one_hot(top_i, N_EXPERTS, dtype=jnp.float32)).astype(h.dtype)
    y = jnp.zeros_like(h)
    for e in range(N_EXPERTS):
        y = y + combine[..., e:e + 1] * swiglu(h, w13[e], w2[e])
    return y


def setup_inputs(seed: int = 0) -> dict:
    key = jax.random.key(seed)
    ks = jax.random.split(key, 24)
    f32 = jnp.float32

    def nrm(k, shape, s):
        return jax.random.normal(k, shape, f32) * s

    D = D_MODEL
    x = nrm(ks[0], (BATCH, SEQ, D), 1.0)
    c = nrm(ks[1], (BATCH, D), 1.0)
    ada_w = nrm(ks[2], (DEPTH, D, 6 * D), 0.02)
    ada_b = nrm(ks[3], (DEPTH, 6 * D), 0.02)
    norm_g = 1.0 + nrm(ks[4], (DEPTH, 2, D), 0.02)
    ml_w_in = nrm(ks[5], (N_A_LAYERS, D, ML_PROJ), D ** -0.5)
    i_b = nrm(ks[6], (N_A_LAYERS, ML_HEADS), 0.1)
    f_b = jnp.linspace(3.0, 6.0, ML_HEADS, dtype=f32) + nrm(ks[7], (N_A_LAYERS, ML_HEADS), 0.1)
    ml_gate_b = jnp.concatenate([i_b, f_b], axis=-1)
    ml_head_g = 1.0 + nrm(ks[8], (N_A_LAYERS, ML_V_W), 0.02)
    ml_w_out = nrm(ks[9], (N_A_LAYERS, ML_V_W, D), ML_V_W ** -0.5)
    ffn_w13 = nrm(ks[10], (N_A_LAYERS, D, 2 * D_FF), D ** -0.5)
    ffn_w2 = nrm(ks[11], (N_A_LAYERS, D_FF, D), D_FF ** -0.5)
    lru_w_in = nrm(ks[12], (N_B_LAYERS, D, 2 * LRU_WIDTH), D ** -0.5)
    lru_conv_w = nrm(ks[13], (N_B_LAYERS, CONV_W, LRU_WIDTH), CONV_W ** -0.5)
    lru_conv_b = nrm(ks[14], (N_B_LAYERS, LRU_WIDTH), 0.02)
    lru_gate_w = nrm(ks[15], (N_B_LAYERS, LRU_HEADS, LRU_BW, 2 * LRU_BW), LRU_BW ** -0.5)
    lru_gate_b = nrm(ks[16], (N_B_LAYERS, LRU_HEADS, 2 * LRU_BW), 0.02)
    a0 = jax.random.uniform(ks[17], (N_B_LAYERS, LRU_WIDTH), f32, 0.9, 0.999)
    p = a0 ** (1.0 / LRU_C)
    lru_a_param = jnp.log(p) - jnp.log1p(-p)
    lru_w_out = nrm(ks[18], (N_B_LAYERS, LRU_WIDTH, D), LRU_WIDTH ** -0.5)
    moe_router_w = nrm(ks[19], (N_B_LAYERS, D, N_EXPERTS), D ** -0.5)
    moe_router_b = nrm(ks[20], (N_B_LAYERS, N_EXPERTS), 0.01)
    moe_w13 = nrm(ks[21], (N_B_LAYERS, N_EXPERTS, D, 2 * D_FF), D ** -0.5)
    moe_w2 = nrm(ks[22], (N_B_LAYERS, N_EXPERTS, D_FF, D), D_FF ** -0.5)
    final_norm_g = 1.0 + nrm(ks[23], (D,), 0.02)
    return {'x': x, 'c': c, 'ada_w': ada_w, 'ada_b': ada_b, 'norm_g': norm_g,
            'ml_w_in': ml_w_in, 'ml_gate_b': ml_gate_b, 'ml_head_g': ml_head_g, 'ml_w_out': ml_w_out,
            'ffn_w13': ffn_w13, 'ffn_w2': ffn_w2,
            'lru_w_in': lru_w_in, 'lru_conv_w': lru_conv_w, 'lru_conv_b': lru_conv_b,
            'lru_gate_w': lru_gate_w, 'lru_gate_b': lru_gate_b, 'lru_a_param': lru_a_param, 'lru_w_out': lru_w_out,
            'moe_router_w': moe_router_w, 'moe_router_b': moe_router_b, 'moe_w13': moe_w13, 'moe_w2': moe_w2,
            'final_norm_g': final_norm_g}


def reference(x, c, ada_w, ada_b, norm_g, ml_w_in, ml_gate_b, ml_head_g, ml_w_out, ffn_w13, ffn_w2,
              lru_w_in, lru_conv_w, lru_conv_b, lru_gate_w, lru_gate_b, lru_a_param, lru_w_out,
              moe_router_w, moe_router_b, moe_w13, moe_w2, final_norm_g):
    c_act = jax.nn.silu(c)
    for i in range(DEPTH):
        mod = c_act @ ada_w[i] + ada_b[i]
        sh1, sc1, g1, sh2, sc2, g2 = jnp.split(mod, 6, axis=-1)
        j = i // N_MIXERS
        h = modulate(rms_norm(x, norm_g[i, 0]), sh1, sc1)
        if i % N_MIXERS == 0:
            mix = mlstm_mixer(h, ml_w_in[j], ml_gate_b[j], ml_head_g[j], ml_w_out[j])
        else:
            mix = rglru_mixer(h, lru_w_in[j], lru_conv_w[j], lru_conv_b[j], lru_gate_w[j],
                              lru_gate_b[j], lru_a_param[j], lru_w_out[j])
        x = x + g1[:, None, :] * mix
        h = modulate(rms_norm(x, norm_g[i, 1]), sh2, sc2)
        if i % 2 == 0:
            ffn = swiglu(h, ffn_w13[j], ffn_w2[j])
        else:
            ffn = moe_swiglu(h, moe_router_w[j], moe_router_b[j], moe_w13[j], moe_w2[j])
        x = x + g2[:, None, :] * ffn
    return rms_norm(x, final_norm_g)
```

```python
import functools

import jax
import jax.numpy as jnp
from jax import lax
from jax.experimental import pallas as pl
from jax.experimental.pallas import tpu as pltpu

F32 = jnp.float32
BF16 = jnp.bfloat16
HIGHEST = lax.Precision.HIGHEST

RMS_EPS = 1e-6
ML_HEADS = 8
ML_DQK = 64
ML_DV = 128
LRU_HEADS = 4
LRU_BW = 256
CONV_W = 4
LRU_C = 8.0
N_EXPERTS = 8

LANES = 128
SUBLANES = 8
VMEM_LIMIT = 56 * 1024 * 1024

ML_CHUNK = 256
LRU_TILE = 256
PROJ_TM = 512
FFN_TM = 1024
FFN_TF = 512
ROUTER_TM = 1024


def _cparams(sem):
    return pltpu.CompilerParams(dimension_semantics=sem, vmem_limit_bytes=VMEM_LIMIT)


def _norm_mod(x, mod_ref, row):
    r = lax.rsqrt(jnp.mean(x * x, axis=-1, keepdims=True) + RMS_EPS)
    return (x * r) * mod_ref[row + 1:row + 2, :] + mod_ref[row:row + 1, :]


def _sigmoid(x):
    return 1.0 / (1.0 + jnp.exp(-x))


def _log_sigmoid(x):
    return jnp.minimum(x, 0.0) - jnp.log1p(jnp.exp(-jnp.abs(x)))


def _ada_kernel(c_ref, w_ref, b_ref, mul_ref, add_ref, o_ref):
    c = c_ref[...]
    ca = (c * _sigmoid(c)).astype(BF16)
    mod = jnp.dot(ca, w_ref[...].astype(BF16), preferred_element_type=F32) + b_ref[...]
    o_ref[...] = mul_ref[...] * mod + add_ref[...]


def _ada(c, ada_w, ada_b, mul, add):
    depth, d, _ = ada_w.shape
    b = c.shape[0]
    row = pl.BlockSpec((None, None, 1, d), lambda i, j: (i, j, 0, 0))
    out = pl.pallas_call(
        _ada_kernel,
        grid=(depth, 6),
        in_specs=[pl.BlockSpec((b, d), lambda i, j: (0, 0)),
                  pl.BlockSpec((None, d, d), lambda i, j: (i, 0, j)),
                  row, row, row],
        out_specs=pl.BlockSpec((None, None, b, d), lambda i, j: (i, j, 0, 0)),
        out_shape=jax.ShapeDtypeStruct((depth, 6, b, d), F32),
        compiler_params=_cparams(("arbitrary", "arbitrary")),
        name="ada_mod",
    )(c, ada_w, ada_b.reshape(depth, 6, 1, d), mul, add)
    return out.transpose(0, 2, 1, 3)


def _mlproj_kernel(x_ref, mod_ref, wq_ref, wkt_ref, wvo_ref, wg_ref, bg_ref,
                   q_ref, kt_ref, vo_ref, g_ref):
    h = _norm_mod(x_ref[...], mod_ref, 0).astype(BF16)
    q = jnp.dot(h, wq_ref[...], preferred_element_type=F32)
    q_ref[...] = (q * (ML_DQK ** -0.5)).astype(BF16)
    kt = lax.dot_general(wkt_ref[...], h, (((1,), (1,)), ((), ())), preferred_element_type=F32)
    kt_ref[...] = kt.astype(BF16)
    vo_ref[...] = jnp.dot(h, wvo_ref[...], preferred_element_type=F32).astype(BF16)
    g_ref[...] = jnp.dot(h, wg_ref[...], preferred_element_type=F32) + bg_ref[...]


def _mlproj(x2, mod, wq, wkt, wvo, wg, bg, seq):
    m, d = x2.shape
    tm = min(PROJ_TM, seq)
    tpb = seq // tm
    nq, nvo = wq.shape[1], wvo.shape[1]
    full = lambda shape: pl.BlockSpec(shape, lambda i: (0,) * len(shape))
    return pl.pallas_call(
        _mlproj_kernel,
        grid=(m // tm,),
        in_specs=[pl.BlockSpec((tm, d), lambda i: (i, 0)),
                  pl.BlockSpec((None, 6, d), lambda i: (i // tpb, 0, 0)),
                  full(wq.shape), full(wkt.shape), full(wvo.shape), full(wg.shape), full(bg.shape)],
        out_specs=[pl.BlockSpec((tm, nq), lambda i: (i, 0)),
                   pl.BlockSpec((nq, tm), lambda i: (0, i)),
                   pl.BlockSpec((tm, nvo), lambda i: (i, 0)),
                   pl.BlockSpec((tm, LANES), lambda i: (i, 0))],
        out_shape=[jax.ShapeDtypeStruct((m, nq), BF16),
                   jax.ShapeDtypeStruct((nq, m), BF16),
                   jax.ShapeDtypeStruct((m, nvo), BF16),
                   jax.ShapeDtypeStruct((m, LANES), F32)],
        compiler_params=_cparams(("parallel",)),
        name="mlstm_proj",
    )(x2, mod, wq, wkt, wvo, wg, bg)


def _gateprep_kernel(g_ref, o_ref):
    g = g_ref[...]
    n = g.shape[0]
    lf = _log_sigmoid(g)
    r = lax.broadcasted_iota(jnp.int32, (n, n), 0)
    s = lax.broadcasted_iota(jnp.int32, (n, n), 1)
    tril = jnp.where(s <= r, 1.0, 0.0).astype(F32)
    b = jnp.dot(tril, lf, precision=HIGHEST, preferred_element_type=F32)
    b_i = pltpu.roll(b, LANES - ML_HEADS, axis=1)
    lane = lax.broadcasted_iota(jnp.int32, g.shape, 1)
    o_ref[...] = jnp.where(lane < ML_HEADS, g - b_i, b)


def _gateprep(gates, chunk):
    m = gates.shape[0]
    return pl.pallas_call(
        _gateprep_kernel,
        grid=(m // chunk,),
        in_specs=[pl.BlockSpec((chunk, LANES), lambda i: (i, 0))],
        out_specs=pl.BlockSpec((chunk, LANES), lambda i: (i, 0)),
        out_shape=jax.ShapeDtypeStruct((m, LANES), F32),
        compiler_params=_cparams(("parallel",)),
        name="mlstm_gateprep",
    )(gates)


def _mlstm_kernel(x_ref, mod_ref, q_ref, kt_ref, vo_ref, arow_ref, col_ref, hg_ref, wout_ref,
                  o_ref, c_scr, m_scr, hs_scr):
    L = q_ref.shape[0]
    vw = ML_HEADS * ML_DV

    @pl.when(pl.program_id(1) == 0)
    def _():
        c_scr[...] = jnp.zeros_like(c_scr)
        m_scr[...] = jnp.zeros_like(m_scr)

    t_idx = lax.broadcasted_iota(jnp.int32, (L, L), 0)
    s_idx = lax.broadcasted_iota(jnp.int32, (L, L), 1)
    causal = s_idx <= t_idx
    lane_q = lax.broadcasted_iota(jnp.int32, (1, 2 * ML_DQK), 1)
    head_mask = [jnp.where(lane_q < ML_DQK, 1.0, 0.0).astype(BF16),
                 jnp.where(lane_q >= ML_DQK, 1.0, 0.0).astype(BF16)]
    lane_v = lax.broadcasted_iota(jnp.int32, (L, ML_DV), 1)
    ones_blk = jnp.where(lane_v == 0, 1.0, 0.0).astype(BF16)

    for h in range(ML_HEADS):
        p, half = h // 2, h % 2
        q2 = q_ref[:, p * 128:(p + 1) * 128]
        qm = q2 * head_mask[half]
        kt2 = kt_ref[p * 128:(p + 1) * 128, :]
        kt_h = kt_ref[h * ML_DQK:(h + 1) * ML_DQK, :]
        v = vo_ref[:, h * ML_DV:(h + 1) * ML_DV]
        og = vo_ref[:, vw + h * ML_DV: vw + (h + 1) * ML_DV]
        v_aug = jnp.concatenate([v, ones_blk], axis=1)
        a_row = arow_ref[h:h + 1, :]
        b_col = col_ref[:, ML_HEADS + h:ML_HEADS + h + 1]
        m_prev = m_scr[h]
        c_prev = c_scr[h]

        a_mat = jnp.where(causal, a_row, -jnp.inf)
        m_col = jnp.maximum(m_prev, jnp.max(a_mat, axis=1, keepdims=True))
        w = jnp.exp(a_mat - m_col)
        e_col = jnp.exp(m_prev - m_col)
        s = jnp.dot(qm, kt2, preferred_element_type=F32) * w
        c_pair = jnp.concatenate([c_prev, c_prev], axis=0).astype(BF16)
        inter = jnp.dot(qm, c_pair, preferred_element_type=F32)
        num_aug = jnp.dot(s.astype(BF16), v_aug, preferred_element_type=F32) + e_col * inter
        num = num_aug[:, :ML_DV]
        den = num_aug[:, ML_DV:ML_DV + 1]
        hh = num / jnp.maximum(jnp.abs(den), jnp.exp(-(b_col + m_col)))
        hn = hh * lax.rsqrt(jnp.mean(hh * hh, axis=-1, keepdims=True) + RMS_EPS)
        hn = hn * hg_ref[:, h * ML_DV:(h + 1) * ML_DV] * _sigmoid(og.astype(F32))
        hs_scr[:, h * ML_DV:(h + 1) * ML_DV] = hn.astype(BF16)

        m_last = m_col[L - 1:L, :]
        w_s = jnp.exp(a_row - m_last)
        decay = jnp.exp(m_prev - m_last)
        kw = (kt_h.astype(F32) * w_s).astype(BF16)
        c_scr[h] = decay * c_prev + jnp.dot(kw, v_aug, preferred_element_type=F32)
        m_scr[h] = b_col[L - 1:L, :] + m_last

    mix = jnp.dot(hs_scr[...], wout_ref[...], preferred_element_type=F32)
    o_ref[...] = x_ref[...] + mod_ref[2:3, :] * mix


def _mlstm(x2, mod, q, kt, vo, a_rows, cols, head_g, w_out, batch, seq):
    m, d = x2.shape
    L = min(ML_CHUNK, seq)
    nc = seq // L
    nq, nvo = q.shape[1], vo.shape[1]
    row_blk = lambda w: pl.BlockSpec((L, w), lambda b, c: (b * nc + c, 0))
    return pl.pallas_call(
        _mlstm_kernel,
        grid=(batch, nc),
        in_specs=[row_blk(d),
                  pl.BlockSpec((None, 6, d), lambda b, c: (b, 0, 0)),
                  row_blk(nq),
                  pl.BlockSpec((nq, L), lambda b, c: (0, b * nc + c)),
                  row_blk(nvo),
                  pl.BlockSpec((ML_HEADS, L), lambda b, c: (0, b * nc + c)),
                  row_blk(LANES),
                  pl.BlockSpec((1, d), lambda b, c: (0, 0)),
                  pl.BlockSpec((d, d), lambda b, c: (0, 0))],
        out_specs=row_blk(d),
        out_shape=jax.ShapeDtypeStruct((m, d), F32),
        scratch_shapes=[pltpu.VMEM((ML_HEADS, ML_DQK, 2 * ML_DV), F32),
                        pltpu.VMEM((ML_HEADS, 1, 1), F32),
                        pltpu.VMEM((L, d), BF16)],
        compiler_params=_cparams(("parallel", "arbitrary")),
        name="mlstm_core",
    )(x2, mod, q, kt, vo, a_rows, cols, head_g, w_out)


def _lru_kernel(x_ref, mod_ref, gx_ref, cw_ref, cb_ref, wgx_ref, wga_ref, bgx_ref, bga_ref, ap_ref,
                wout_ref, o_ref, xbuf, a_scr, u_scr, h_scr, hc_scr):
    T, W = h_scr.shape
    PAD = SUBLANES

    @pl.when(pl.program_id(1) == 0)
    def _():
        xbuf[0:PAD, :] = jnp.zeros((PAD, W), F32)
        hc_scr[...] = jnp.zeros_like(hc_scr)

    xbuf[PAD:PAD + T, :] = gx_ref[:, W:2 * W].astype(F32)
    xc = cb_ref[...] + xbuf[PAD - 3:PAD - 3 + T, :] * cw_ref[0:1, :]
    for j in range(1, CONV_W):
        xc = xc + xbuf[PAD - 3 + j:PAD - 3 + j + T, :] * cw_ref[j:j + 1, :]
    xbuf[0:PAD, :] = xbuf[T:T + PAD, :]

    xcb = xc.astype(BF16)
    sp = ap_ref[...]
    sp = jnp.maximum(-sp, 0.0) + jnp.log1p(jnp.exp(-jnp.abs(sp)))
    for hd in range(LRU_HEADS):
        sl = slice(hd * LRU_BW, (hd + 1) * LRU_BW)
        xh = xcb[:, sl]
        gxh = _sigmoid(jnp.dot(xh, wgx_ref[hd], preferred_element_type=F32) + bgx_ref[:, sl])
        gah = _sigmoid(jnp.dot(xh, wga_ref[hd], preferred_element_type=F32) + bga_ref[:, sl])
        log_a = (-LRU_C) * gah * sp[:, sl]
        a = jnp.exp(log_a)
        a_scr[:, sl] = a
        u_scr[:, sl] = xc[:, sl] * gxh * jnp.sqrt(jnp.tanh(-log_a) * (1.0 + a * a))

    row = lax.broadcasted_iota(jnp.int32, (SUBLANES, W), 0)

    def body(r, hc):
        base = pl.multiple_of(r * SUBLANES, SUBLANES)
        a8 = a_scr[pl.ds(base, SUBLANES), :]
        u8 = u_scr[pl.ds(base, SUBLANES), :]
        for sft in (1, 2, 4):
            a_sh = pltpu.roll(a8, sft, axis=0)
            u_sh = pltpu.roll(u8, sft, axis=0)
            valid = row >= sft
            u8 = jnp.where(valid, a8 * u_sh + u8, u8)
            a8 = jnp.where(valid, a8 * a_sh, a8)
        h8 = u8 + a8 * hc
        h_scr[pl.ds(base, SUBLANES), :] = h8
        return h8[SUBLANES - 1:SUBLANES, :]

    hc_scr[...] = lax.fori_loop(0, T // SUBLANES, body, hc_scr[...], unroll=2)

    gb = gx_ref[:, 0:W].astype(F32)
    y = (h_scr[...] * jax.nn.gelu(gb)).astype(BF16)
    mix = jnp.dot(y, wout_ref[...], preferred_element_type=F32)
    o_ref[...] = x_ref[...] + mod_ref[2:3, :] * mix


def _lru(x2, mod, gx, conv_w, conv_b, wgx, wga, bgx, bga, a_param, w_out, batch, seq):
    m, d = x2.shape
    w = w_out.shape[0]
    T = min(LRU_TILE, seq)
    nt = seq // T
    row_blk = lambda wd: pl.BlockSpec((T, wd), lambda b, t: (b * nt + t, 0))
    full = lambda a: pl.BlockSpec(a.shape, lambda b, t: (0,) * a.ndim)
    return pl.pallas_call(
        _lru_kernel,
        grid=(batch, nt),
        in_specs=[row_blk(d),
                  pl.BlockSpec((None, 6, d), lambda b, t: (b, 0, 0)),
                  row_blk(2 * w),
                  full(conv_w), full(conv_b), full(wgx), full(wga), full(bgx), full(bga), full(a_param),
                  full(w_out)],
        out_specs=row_blk(d),
        out_shape=jax.ShapeDtypeStruct((m, d), F32),
        scratch_shapes=[pltpu.VMEM((T + SUBLANES, w), F32),
                        pltpu.VMEM((T, w), F32),
                        pltpu.VMEM((T, w), F32),
                        pltpu.VMEM((T, w), F32),
                        pltpu.VMEM((1, w), F32)],
        compiler_params=_cparams(("parallel", "arbitrary")),
        name="rglru_core",
    )(x2, mod, gx, conv_w, conv_b, wgx, wga, bgx, bga, a_param, w_out)


def _lruproj_kernel(x_ref, mod_ref, w_ref, o_ref):
    h = _norm_mod(x_ref[...], mod_ref, 0).astype(BF16)
    o_ref[...] = jnp.dot(h, w_ref[...], preferred_element_type=F32).astype(BF16)


def _lruproj(x2, mod, w, seq):
    m, d = x2.shape
    n = w.shape[1]
    tm = min(PROJ_TM, seq)
    tpb = seq // tm
    return pl.pallas_call(
        _lruproj_kernel,
        grid=(m // tm,),
        in_specs=[pl.BlockSpec((tm, d), lambda i: (i, 0)),
                  pl.BlockSpec((None, 6, d), lambda i: (i // tpb, 0, 0)),
                  pl.BlockSpec((d, n), lambda i: (0, 0))],
        out_specs=pl.BlockSpec((tm, n), lambda i: (i, 0)),
        out_shape=jax.ShapeDtypeStruct((m, n), BF16),
        compiler_params=_cparams(("parallel",)),
        name="rglru_proj",
    )(x2, mod, w)


def _final_norm(x, g_ref):
    return x * lax.rsqrt(jnp.mean(x * x, axis=-1, keepdims=True) + RMS_EPS) * g_ref[...]


def _ffn_kernel(x_ref, mod_ref, w1_ref, w3_ref, w2_ref, o_ref, h_scr, acc_scr):
    f = pl.program_id(1)

    @pl.when(f == 0)
    def _():
        h_scr[...] = _norm_mod(x_ref[...], mod_ref, 3).astype(BF16)
        acc_scr[...] = jnp.zeros_like(acc_scr)

    h = h_scr[...]
    a = jnp.dot(h, w1_ref[...], preferred_element_type=F32)
    b = jnp.dot(h, w3_ref[...], preferred_element_type=F32)
    g = (a * _sigmoid(a) * b).astype(BF16)
    acc_scr[...] += jnp.dot(g, w2_ref[...], preferred_element_type=F32)

    @pl.when(f == pl.num_programs(1) - 1)
    def _():
        o_ref[...] = x_ref[...] + mod_ref[5:6, :] * acc_scr[...]


def _ffn(x2, mod, w13, w2, seq):
    m, d = x2.shape
    dff = w2.shape[0]
    tm = min(FFN_TM, seq)
    tf = FFN_TF
    tpb = seq // tm
    nf = dff // tf
    return pl.pallas_call(
        _ffn_kernel,
        grid=(m // tm, nf),
        in_specs=[pl.BlockSpec((tm, d), lambda i, f: (i, 0)),
                  pl.BlockSpec((None, 6, d), lambda i, f: (i // tpb, 0, 0)),
                  pl.BlockSpec((d, tf), lambda i, f: (0, f)),
                  pl.BlockSpec((d, tf), lambda i, f: (0, nf + f)),
                  pl.BlockSpec((tf, d), lambda i, f: (f, 0))],
        out_specs=pl.BlockSpec((tm, d), lambda i, f: (i, 0)),
        out_shape=jax.ShapeDtypeStruct((m, d), F32),
        scratch_shapes=[pltpu.VMEM((tm, d), BF16), pltpu.VMEM((tm, d), F32)],
        compiler_params=_cparams(("parallel", "arbitrary")),
        name="ffn_swiglu",
    )(x2, mod, w13, w13, w2)


def _router_kernel(x_ref, mod_ref, rw_ref, rb_ref, h_ref, comb_ref):
    h = _norm_mod(x_ref[...], mod_ref, 3)
    h_ref[...] = h.astype(BF16)
    logits = jnp.dot(h, rw_ref[...], precision=HIGHEST, preferred_element_type=F32) + rb_ref[...]
    lane = lax.broadcasted_iota(jnp.int32, logits.shape, 1).astype(F32)
    v1 = jnp.max(logits, axis=1, keepdims=True)
    i1 = jnp.min(jnp.where(logits == v1, lane, float(LANES)), axis=1, keepdims=True)
    rest = jnp.where(lane == i1, -jnp.inf, logits)
    v2 = jnp.max(rest, axis=1, keepdims=True)
    i2 = jnp.min(jnp.where(rest == v2, lane, float(LANES)), axis=1, keepdims=True)
    e2 = jnp.exp(v2 - v1)
    w1 = 1.0 / (1.0 + e2)
    w2 = e2 / (1.0 + e2)
    comb_ref[...] = jnp.where(lane == i1, w1, 0.0) + jnp.where(lane == i2, w2, 0.0)


def _router(x2, mod, rw, rb, seq):
    m, d = x2.shape
    tm = min(ROUTER_TM, seq)
    tpb = seq // tm
    return pl.pallas_call(
        _router_kernel,
        grid=(m // tm,),
        in_specs=[pl.BlockSpec((tm, d), lambda i: (i, 0)),
                  pl.BlockSpec((None, 6, d), lambda i: (i // tpb, 0, 0)),
                  pl.BlockSpec((d, LANES), lambda i: (0, 0)),
                  pl.BlockSpec((1, LANES), lambda i: (0, 0))],
        out_specs=[pl.BlockSpec((tm, d), lambda i: (i, 0)),
                   pl.BlockSpec((tm, LANES), lambda i: (i, 0))],
        out_shape=[jax.ShapeDtypeStruct((m, d), BF16),
                   jax.ShapeDtypeStruct((m, LANES), F32)],
        compiler_params=_cparams(("parallel",)),
        name="moe_router",
    )(x2, mod, rw, rb)


def _moe_kernel(x_ref, mod_ref, h_ref, comb_ref, w1_ref, w3_ref, w2_ref, fg_ref, o_ref, cw_scr, acc_scr,
                *, final):
    e = pl.program_id(1)
    f = pl.program_id(2)

    @pl.when((e == 0) & (f == 0))
    def _():
        acc_scr[...] = jnp.zeros_like(acc_scr)

    @pl.when(f == 0)
    def _():
        comb = comb_ref[...]
        lane = lax.broadcasted_iota(jnp.int32, comb.shape, 1)
        cw_scr[...] = jnp.sum(jnp.where(lane == e, comb, 0.0), axis=1, keepdims=True)

    h = h_ref[...]
    a = jnp.dot(h, w1_ref[...], preferred_element_type=F32)
    b = jnp.dot(h, w3_ref[...], preferred_element_type=F32)
    g = (a * _sigmoid(a) * b).astype(BF16)
    acc_scr[...] += cw_scr[...] * jnp.dot(g, w2_ref[...], preferred_element_type=F32)

    @pl.when((e == pl.num_programs(1) - 1) & (f == pl.num_programs(2) - 1))
    def _():
        y = x_ref[...] + mod_ref[5:6, :] * acc_scr[...]
        o_ref[...] = _final_norm(y, fg_ref) if final else y


def _moe(x2, mod, h, comb, w13, w2, final_g, final, seq):
    m, d = x2.shape
    ne, dff, _ = w2.shape
    tm = min(FFN_TM, seq)
    tf = FFN_TF
    tpb = seq // tm
    nf = dff // tf
    return pl.pallas_call(
        functools.partial(_moe_kernel, final=final),
        grid=(m // tm, ne, nf),
        in_specs=[pl.BlockSpec((tm, d), lambda i, e, f: (i, 0)),
                  pl.BlockSpec((None, 6, d), lambda i, e, f: (i // tpb, 0, 0)),
                  pl.BlockSpec((tm, d), lambda i, e, f: (i, 0)),
                  pl.BlockSpec((tm, LANES), lambda i, e, f: (i, 0)),
                  pl.BlockSpec((None, d, tf), lambda i, e, f: (e, 0, f)),
                  pl.BlockSpec((None, d, tf), lambda i, e, f: (e, 0, nf + f)),
                  pl.BlockSpec((None, tf, d), lambda i, e, f: (e, f, 0)),
                  pl.BlockSpec((1, d), lambda i, e, f: (0, 0))],
        out_specs=pl.BlockSpec((tm, d), lambda i, e, f: (i, 0)),
        out_shape=jax.ShapeDtypeStruct((m, d), F32),
        scratch_shapes=[pltpu.VMEM((tm, 1), F32), pltpu.VMEM((tm, d), F32)],
        compiler_params=_cparams(("parallel", "arbitrary", "arbitrary")),
        name="moe_experts",
    )(x2, mod, h, comb, w13, w13, w2, final_g)


def kernel(x, c, ada_w, ada_b, norm_g, ml_w_in, ml_gate_b, ml_head_g, ml_w_out, ffn_w13, ffn_w2,
           lru_w_in, lru_conv_w, lru_conv_b, lru_gate_w, lru_gate_b, lru_a_param, lru_w_out,
           moe_router_w, moe_router_b, moe_w13, moe_w2, final_norm_g):
    batch, seq, d = x.shape
    depth = ada_w.shape[0]
    m = batch * seq
    qkw = ML_HEADS * ML_DQK
    vw = ML_HEADS * ML_DV

    ones = jnp.ones((depth, 1, d), F32)
    zeros = jnp.zeros((depth, 1, d), F32)
    g0, g1 = norm_g[:, 0:1, :], norm_g[:, 1:2, :]
    mul = jnp.concatenate([ones, g0, ones, ones, g1, ones], axis=1).reshape(depth, 6, 1, d)
    add = jnp.concatenate([zeros, g0, zeros, zeros, g1, zeros], axis=1).reshape(depth, 6, 1, d)
    mods = _ada(c, ada_w, ada_b, mul, add)

    x2 = x.reshape(m, d)
    for i in range(depth):
        j = i // 2
        mod = mods[i]
        last = i == depth - 1
        if i % 2 == 0:
            w_in = ml_w_in[j]
            wq = w_in[:, :qkw].astype(BF16)
            wkt = w_in[:, qkw:2 * qkw].T.astype(BF16)
            wvo = w_in[:, 2 * qkw:2 * qkw + 2 * vw].astype(BF16)
            wg = jnp.pad(w_in[:, 2 * qkw + 2 * vw:], ((0, 0), (0, LANES - 2 * ML_HEADS))).astype(BF16)
            bg = jnp.pad(ml_gate_b[j], (0, LANES - 2 * ML_HEADS)).reshape(1, LANES)
            q, kt, vo, gates = _mlproj(x2, mod, wq, wkt, wvo, wg, bg, seq)
            cols = _gateprep(gates, min(ML_CHUNK, seq))
            a_rows = cols[:, :ML_HEADS].T
            x2 = _mlstm(x2, mod, q, kt, vo, a_rows, cols, ml_head_g[j].reshape(1, vw),
                        ml_w_out[j].astype(BF16), batch, seq)
            x2 = _ffn(x2, mod, ffn_w13[j].astype(BF16), ffn_w2[j].astype(BF16), seq)
        else:
            gx = _lruproj(x2, mod, lru_w_in[j].astype(BF16), seq)
            gw = lru_gate_w[j]
            gb = lru_gate_b[j]
            wgx = gw[:, :, :LRU_BW].astype(BF16)
            wga = gw[:, :, LRU_BW:].astype(BF16)
            bgx = gb[:, :LRU_BW].reshape(1, -1)
            bga = gb[:, LRU_BW:].reshape(1, -1)
            x2 = _lru(x2, mod, gx, lru_conv_w[j], lru_conv_b[j].reshape(1, -1), wgx, wga, bgx, bga,
                      lru_a_param[j].reshape(1, -1), lru_w_out[j].astype(BF16), batch, seq)
            rw = jnp.pad(moe_router_w[j], ((0, 0), (0, LANES - N_EXPERTS)))
            rb = jnp.pad(moe_router_b[j], (0, LANES - N_EXPERTS), constant_values=-jnp.inf).reshape(1, LANES)
            hb, comb = _router(x2, mod, rw, rb, seq)
            x2 = _moe(x2, mod, hb, comb, moe_w13[j].astype(BF16), moe_w2[j].astype(BF16),
                      final_norm_g.reshape(1, d), last, seq)
    return x2.reshape(batch, seq, d)
```

```python
import functools

import jax
import jax.numpy as jnp
from jax import lax
from jax.experimental import pallas as pl
from jax.experimental.pallas import tpu as pltpu

F32 = jnp.float32
BF16 = jnp.bfloat16
HIGHEST = lax.Precision.HIGHEST

RMS_EPS = 1e-6
ML_HEADS = 8
ML_DQK = 64
ML_DV = 128
LRU_HEADS = 4
LRU_BW = 256
CONV_W = 4
LRU_C = 8.0
N_EXPERTS = 8

LANES = 128
SUBLANES = 8
VMEM_LIMIT = 56 * 1024 * 1024

ML_CHUNK = 256
LRU_TILE = 256
PROJ_TM = 512
FFN_TM = 1024
FFN_TF = 512
ROUTER_TM = 1024
MOE_META_T = 512
MOE_DMA_T = 1024


def _cparams(sem):
    return pltpu.CompilerParams(dimension_semantics=sem, vmem_limit_bytes=VMEM_LIMIT)


def _norm_mod(x, mod_ref, row):
    r = lax.rsqrt(jnp.mean(x * x, axis=-1, keepdims=True) + RMS_EPS)
    return (x * r) * mod_ref[row + 1:row + 2, :] + mod_ref[row:row + 1, :]


def _sigmoid(x):
    return 1.0 / (1.0 + jnp.exp(-x))


def _log_sigmoid(x):
    return jnp.minimum(x, 0.0) - jnp.log1p(jnp.exp(-jnp.abs(x)))


def _ada_kernel(c_ref, w_ref, b_ref, mul_ref, add_ref, o_ref):
    c = c_ref[...]
    ca = (c * _sigmoid(c)).astype(BF16)
    mod = jnp.dot(ca, w_ref[...].astype(BF16), preferred_element_type=F32) + b_ref[...]
    o_ref[...] = mul_ref[...] * mod + add_ref[...]


def _ada(c, ada_w, ada_b, mul, add):
    depth, d, _ = ada_w.shape
    b = c.shape[0]
    row = pl.BlockSpec((None, None, 1, d), lambda i, j: (i, j, 0, 0))
    out = pl.pallas_call(
        _ada_kernel,
        grid=(depth, 6),
        in_specs=[pl.BlockSpec((b, d), lambda i, j: (0, 0)),
                  pl.BlockSpec((None, d, d), lambda i, j: (i, 0, j)),
                  row, row, row],
        out_specs=pl.BlockSpec((None, None, b, d), lambda i, j: (i, j, 0, 0)),
        out_shape=jax.ShapeDtypeStruct((depth, 6, b, d), F32),
        compiler_params=_cparams(("arbitrary", "arbitrary")),
        name="ada_mod",
    )(c, ada_w, ada_b.reshape(depth, 6, 1, d), mul, add)
    return out.transpose(0, 2, 1, 3)


def _mlproj_kernel(x_ref, mod_ref, wq_ref, wkt_ref, wvo_ref, wg_ref, bg_ref,
                   q_ref, kt_ref, vo_ref, g_ref):
    h = _norm_mod(x_ref[...], mod_ref, 0).astype(BF16)
    q = jnp.dot(h, wq_ref[...], preferred_element_type=F32)
    q_ref[...] = (q * (ML_DQK ** -0.5)).astype(BF16)
    kt = lax.dot_general(wkt_ref[...], h, (((1,), (1,)), ((), ())), preferred_element_type=F32)
    kt_ref[...] = kt.astype(BF16)
    vo_ref[...] = jnp.dot(h, wvo_ref[...], preferred_element_type=F32).astype(BF16)
    g_ref[...] = jnp.dot(h, wg_ref[...], preferred_element_type=F32) + bg_ref[...]


def _mlproj(x2, mod, wq, wkt, wvo, wg, bg, seq):
    m, d = x2.shape
    tm = min(PROJ_TM, seq)
    tpb = seq // tm
    nq, nvo = wq.shape[1], wvo.shape[1]
    full = lambda shape: pl.BlockSpec(shape, lambda i: (0,) * len(shape))
    return pl.pallas_call(
        _mlproj_kernel,
        grid=(m // tm,),
        in_specs=[pl.BlockSpec((tm, d), lambda i: (i, 0)),
                  pl.BlockSpec((None, 6, d), lambda i: (i // tpb, 0, 0)),
                  full(wq.shape), full(wkt.shape), full(wvo.shape), full(wg.shape), full(bg.shape)],
        out_specs=[pl.BlockSpec((tm, nq), lambda i: (i, 0)),
                   pl.BlockSpec((nq, tm), lambda i: (0, i)),
                   pl.BlockSpec((tm, nvo), lambda i: (i, 0)),
                   pl.BlockSpec((tm, LANES), lambda i: (i, 0))],
        out_shape=[jax.ShapeDtypeStruct((m, nq), BF16),
                   jax.ShapeDtypeStruct((nq, m), BF16),
                   jax.ShapeDtypeStruct((m, nvo), BF16),
                   jax.ShapeDtypeStruct((m, LANES), F32)],
        compiler_params=_cparams(("parallel",)),
        name="mlstm_proj",
    )(x2, mod, wq, wkt, wvo, wg, bg)


def _gateprep_kernel(g_ref, o_ref):
    g = g_ref[...]
    n = g.shape[0]
    lf = _log_sigmoid(g)
    r = lax.broadcasted_iota(jnp.int32, (n, n), 0)
    s = lax.broadcasted_iota(jnp.int32, (n, n), 1)
    tril = jnp.where(s <= r, 1.0, 0.0).astype(F32)
    b = jnp.dot(tril, lf, precision=HIGHEST, preferred_element_type=F32)
    b_i = pltpu.roll(b, LANES - ML_HEADS, axis=1)
    lane = lax.broadcasted_iota(jnp.int32, g.shape, 1)
    o_ref[...] = jnp.where(lane < ML_HEADS, g - b_i, b)


def _gateprep(gates, chunk):
    m = gates.shape[0]
    return pl.pallas_call(
        _gateprep_kernel,
        grid=(m // chunk,),
        in_specs=[pl.BlockSpec((chunk, LANES), lambda i: (i, 0))],
        out_specs=pl.BlockSpec((chunk, LANES), lambda i: (i, 0)),
        out_shape=jax.ShapeDtypeStruct((m, LANES), F32),
        compiler_params=_cparams(("parallel",)),
        name="mlstm_gateprep",
    )(gates)


def _mlstm_kernel(x_ref, mod_ref, q_ref, kt_ref, vo_ref, arow_ref, col_ref, hg_ref, wout_ref,
                  o_ref, c_scr, m_scr, hs_scr):
    L = q_ref.shape[0]
    vw = ML_HEADS * ML_DV

    @pl.when(pl.program_id(1) == 0)
    def _():
        c_scr[...] = jnp.zeros_like(c_scr)
        m_scr[...] = jnp.zeros_like(m_scr)

    t_idx = lax.broadcasted_iota(jnp.int32, (L, L), 0)
    s_idx = lax.broadcasted_iota(jnp.int32, (L, L), 1)
    causal = s_idx <= t_idx
    lane_q = lax.broadcasted_iota(jnp.int32, (1, 2 * ML_DQK), 1)
    head_mask = [jnp.where(lane_q < ML_DQK, 1.0, 0.0).astype(BF16),
                 jnp.where(lane_q >= ML_DQK, 1.0, 0.0).astype(BF16)]
    lane_v = lax.broadcasted_iota(jnp.int32, (L, ML_DV), 1)
    ones_blk = jnp.where(lane_v == 0, 1.0, 0.0).astype(BF16)

    for h in range(ML_HEADS):
        p, half = h // 2, h % 2
        q2 = q_ref[:, p * 128:(p + 1) * 128]
        qm = q2 * head_mask[half]
        kt2 = kt_ref[p * 128:(p + 1) * 128, :]
        kt_h = kt_ref[h * ML_DQK:(h + 1) * ML_DQK, :]
        v = vo_ref[:, h * ML_DV:(h + 1) * ML_DV]
        og = vo_ref[:, vw + h * ML_DV: vw + (h + 1) * ML_DV]
        v_aug = jnp.concatenate([v, ones_blk], axis=1)
        a_row = arow_ref[h:h + 1, :]
        b_col = col_ref[:, ML_HEADS + h:ML_HEADS + h + 1]
        m_prev = m_scr[h]
        c_prev = c_scr[h]

        a_mat = jnp.where(causal, a_row, -jnp.inf)
        m_col = jnp.maximum(m_prev, jnp.max(a_mat, axis=1, keepdims=True))
        w = jnp.exp(a_mat - m_col)
        e_col = jnp.exp(m_prev - m_col)
        s = jnp.dot(qm, kt2, preferred_element_type=F32) * w
        c_pair = jnp.concatenate([c_prev, c_prev], axis=0).astype(BF16)
        inter = jnp.dot(qm, c_pair, preferred_element_type=F32)
        num_aug = jnp.dot(s.astype(BF16), v_aug, preferred_element_type=F32) + e_col * inter
        num = num_aug[:, :ML_DV]
        den = num_aug[:, ML_DV:ML_DV + 1]
        hh = num / jnp.maximum(jnp.abs(den), jnp.exp(-(b_col + m_col)))
        hn = hh * lax.rsqrt(jnp.mean(hh * hh, axis=-1, keepdims=True) + RMS_EPS)
        hn = hn * hg_ref[:, h * ML_DV:(h + 1) * ML_DV] * _sigmoid(og.astype(F32))
        hs_scr[:, h * ML_DV:(h + 1) * ML_DV] = hn.astype(BF16)

        m_last = m_col[L - 1:L, :]
        w_s = jnp.exp(a_row - m_last)
        decay = jnp.exp(m_prev - m_last)
        kw = (kt_h.astype(F32) * w_s).astype(BF16)
        c_scr[h] = decay * c_prev + jnp.dot(kw, v_aug, preferred_element_type=F32)
        m_scr[h] = b_col[L - 1:L, :] + m_last

    mix = jnp.dot(hs_scr[...], wout_ref[...], preferred_element_type=F32)
    o_ref[...] = x_ref[...] + mod_ref[2:3, :] * mix


def _mlstm(x2, mod, q, kt, vo, a_rows, cols, head_g, w_out, batch, seq):
    m, d = x2.shape
    L = min(ML_CHUNK, seq)
    nc = seq // L
    nq, nvo = q.shape[1], vo.shape[1]
    row_blk = lambda w: pl.BlockSpec((L, w), lambda b, c: (b * nc + c, 0))
    return pl.pallas_call(
        _mlstm_kernel,
        grid=(batch, nc),
        in_specs=[row_blk(d),
                  pl.BlockSpec((None, 6, d), lambda b, c: (b, 0, 0)),
                  row_blk(nq),
                  pl.BlockSpec((nq, L), lambda b, c: (0, b * nc + c)),
                  row_blk(nvo),
                  pl.BlockSpec((ML_HEADS, L), lambda b, c: (0, b * nc + c)),
                  row_blk(LANES),
                  pl.BlockSpec((1, d), lambda b, c: (0, 0)),
                  pl.BlockSpec((d, d), lambda b, c: (0, 0))],
        out_specs=row_blk(d),
        out_shape=jax.ShapeDtypeStruct((m, d), F32),
        scratch_shapes=[pltpu.VMEM((ML_HEADS, ML_DQK, 2 * ML_DV), F32),
                        pltpu.VMEM((ML_HEADS, 1, 1), F32),
                        pltpu.VMEM((L, d), BF16)],
        compiler_params=_cparams(("parallel", "arbitrary")),
        name="mlstm_core",
    )(x2, mod, q, kt, vo, a_rows, cols, head_g, w_out)


def _lru_kernel(x_ref, mod_ref, gx_ref, cw_ref, cb_ref, wgx_ref, wga_ref, bgx_ref, bga_ref, ap_ref,
                wout_ref, o_ref, xbuf, a_scr, u_scr, h_scr, hc_scr):
    T, W = h_scr.shape
    PAD = SUBLANES

    @pl.when(pl.program_id(1) == 0)
    def _():
        xbuf[0:PAD, :] = jnp.zeros((PAD, W), F32)
        hc_scr[...] = jnp.zeros_like(hc_scr)

    xbuf[PAD:PAD + T, :] = gx_ref[:, W:2 * W].astype(F32)
    xc = cb_ref[...] + xbuf[PAD - 3:PAD - 3 + T, :] * cw_ref[0:1, :]
    for j in range(1, CONV_W):
        xc = xc + xbuf[PAD - 3 + j:PAD - 3 + j + T, :] * cw_ref[j:j + 1, :]
    xbuf[0:PAD, :] = xbuf[T:T + PAD, :]

    xcb = xc.astype(BF16)
    sp = ap_ref[...]
    sp = jnp.maximum(-sp, 0.0) + jnp.log1p(jnp.exp(-jnp.abs(sp)))
    for hd in range(LRU_HEADS):
        sl = slice(hd * LRU_BW, (hd + 1) * LRU_BW)
        xh = xcb[:, sl]
        gxh = _sigmoid(jnp.dot(xh, wgx_ref[hd], preferred_element_type=F32) + bgx_ref[:, sl])
        gah = _sigmoid(jnp.dot(xh, wga_ref[hd], preferred_element_type=F32) + bga_ref[:, sl])
        log_a = (-LRU_C) * gah * sp[:, sl]
        a = jnp.exp(log_a)
        a_scr[:, sl] = a
        u_scr[:, sl] = xc[:, sl] * gxh * jnp.sqrt(jnp.tanh(-log_a) * (1.0 + a * a))

    row = lax.broadcasted_iota(jnp.int32, (SUBLANES, W), 0)

    def body(r, hc):
        base = pl.multiple_of(r * SUBLANES, SUBLANES)
        a8 = a_scr[pl.ds(base, SUBLANES), :]
        u8 = u_scr[pl.ds(base, SUBLANES), :]
        for sft in (1, 2, 4):
            a_sh = pltpu.roll(a8, sft, axis=0)
            u_sh = pltpu.roll(u8, sft, axis=0)
            valid = row >= sft
            u8 = jnp.where(valid, a8 * u_sh + u8, u8)
            a8 = jnp.where(valid, a8 * a_sh, a8)
        h8 = u8 + a8 * hc
        h_scr[pl.ds(base, SUBLANES), :] = h8
        return h8[SUBLANES - 1:SUBLANES, :]

    hc_scr[...] = lax.fori_loop(0, T // SUBLANES, body, hc_scr[...], unroll=2)

    gb = gx_ref[:, 0:W].astype(F32)
    y = (h_scr[...] * jax.nn.gelu(gb)).astype(BF16)
    mix = jnp.dot(y, wout_ref[...], preferred_element_type=F32)
    o_ref[...] = x_ref[...] + mod_ref[2:3, :] * mix


def _lru(x2, mod, gx, conv_w, conv_b, wgx, wga, bgx, bga, a_param, w_out, batch, seq):
    m, d = x2.shape
    w = w_out.shape[0]
    T = min(LRU_TILE, seq)
    nt = seq // T
    row_blk = lambda wd: pl.BlockSpec((T, wd), lambda b, t: (b * nt + t, 0))
    full = lambda a: pl.BlockSpec(a.shape, lambda b, t: (0,) * a.ndim)
    return pl.pallas_call(
        _lru_kernel,
        grid=(batch, nt),
        in_specs=[row_blk(d),
                  pl.BlockSpec((None, 6, d), lambda b, t: (b, 0, 0)),
                  row_blk(2 * w),
                  full(conv_w), full(conv_b), full(wgx), full(wga), full(bgx), full(bga), full(a_param),
                  full(w_out)],
        out_specs=row_blk(d),
        out_shape=jax.ShapeDtypeStruct((m, d), F32),
        scratch_shapes=[pltpu.VMEM((T + SUBLANES, w), F32),
                        pltpu.VMEM((T, w), F32),
                        pltpu.VMEM((T, w), F32),
                        pltpu.VMEM((T, w), F32),
                        pltpu.VMEM((1, w), F32)],
        compiler_params=_cparams(("parallel", "arbitrary")),
        name="rglru_core",
    )(x2, mod, gx, conv_w, conv_b, wgx, wga, bgx, bga, a_param, w_out)


def _lruproj_kernel(x_ref, mod_ref, w_ref, o_ref):
    h = _norm_mod(x_ref[...], mod_ref, 0).astype(BF16)
    o_ref[...] = jnp.dot(h, w_ref[...], preferred_element_type=F32).astype(BF16)


def _lruproj(x2, mod, w, seq):
    m, d = x2.shape
    n = w.shape[1]
    tm = min(PROJ_TM, seq)
    tpb = seq // tm
    return pl.pallas_call(
        _lruproj_kernel,
        grid=(m // tm,),
        in_specs=[pl.BlockSpec((tm, d), lambda i: (i, 0)),
                  pl.BlockSpec((None, 6, d), lambda i: (i // tpb, 0, 0)),
                  pl.BlockSpec((d, n), lambda i: (0, 0))],
        out_specs=pl.BlockSpec((tm, n), lambda i: (i, 0)),
        out_shape=jax.ShapeDtypeStruct((m, n), BF16),
        compiler_params=_cparams(("parallel",)),
        name="rglru_proj",
    )(x2, mod, w)


def _final_norm(x, g_ref):
    return x * lax.rsqrt(jnp.mean(x * x, axis=-1, keepdims=True) + RMS_EPS) * g_ref[...]


def _ffn_kernel(x_ref, mod_ref, w1_ref, w3_ref, w2_ref, o_ref, h_scr, acc_scr):
    f = pl.program_id(1)

    @pl.when(f == 0)
    def _():
        h_scr[...] = _norm_mod(x_ref[...], mod_ref, 3).astype(BF16)
        acc_scr[...] = jnp.zeros_like(acc_scr)

    h = h_scr[...]
    a = jnp.dot(h, w1_ref[...], preferred_element_type=F32)
    b = jnp.dot(h, w3_ref[...], preferred_element_type=F32)
    g = (a * _sigmoid(a) * b).astype(BF16)
    acc_scr[...] += jnp.dot(g, w2_ref[...], preferred_element_type=F32)

    @pl.when(f == pl.num_programs(1) - 1)
    def _():
        o_ref[...] = x_ref[...] + mod_ref[5:6, :] * acc_scr[...]


def _ffn(x2, mod, w13, w2, seq):
    m, d = x2.shape
    dff = w2.shape[0]
    tm = min(FFN_TM, seq)
    tf = FFN_TF
    tpb = seq // tm
    nf = dff // tf
    return pl.pallas_call(
        _ffn_kernel,
        grid=(m // tm, nf),
        in_specs=[pl.BlockSpec((tm, d), lambda i, f: (i, 0)),
                  pl.BlockSpec((None, 6, d), lambda i, f: (i // tpb, 0, 0)),
                  pl.BlockSpec((d, tf), lambda i, f: (0, f)),
                  pl.BlockSpec((d, tf), lambda i, f: (0, nf + f)),
                  pl.BlockSpec((tf, d), lambda i, f: (f, 0))],
        out_specs=pl.BlockSpec((tm, d), lambda i, f: (i, 0)),
        out_shape=jax.ShapeDtypeStruct((m, d), F32),
        scratch_shapes=[pltpu.VMEM((tm, d), BF16), pltpu.VMEM((tm, d), F32)],
        compiler_params=_cparams(("parallel", "arbitrary")),
        name="ffn_swiglu",
    )(x2, mod, w13, w13, w2)


def _router_kernel(x_ref, mod_ref, rw_ref, rb_ref, h_ref, info_ref, info_t_ref):
    h = _norm_mod(x_ref[...], mod_ref, 3)
    h_ref[...] = h
    logits = jnp.dot(h, rw_ref[...], precision=HIGHEST, preferred_element_type=F32) + rb_ref[...]
    lane = lax.broadcasted_iota(jnp.int32, logits.shape, 1).astype(F32)
    v1 = jnp.max(logits, axis=1, keepdims=True)
    i1 = jnp.min(jnp.where(logits == v1, lane, float(LANES)), axis=1, keepdims=True)
    rest = jnp.where(lane == i1, -jnp.inf, logits)
    v2 = jnp.max(rest, axis=1, keepdims=True)
    i2 = jnp.min(jnp.where(rest == v2, lane, float(LANES)), axis=1, keepdims=True)
    e2 = jnp.exp(v2 - v1)
    w1 = 1.0 / (1.0 + e2)
    w2 = e2 / (1.0 + e2)
    info = jnp.where(lane == 0.0, w1, jnp.where(lane == 1.0, w2, jnp.where(lane == 2.0, i1,
                                                                           jnp.where(lane == 3.0, i2, 0.0))))
    info_ref[...] = info
    info_t_ref[...] = info.T[0:SUBLANES, :]


def _router(x2, mod, rw, rb, seq):
    m, d = x2.shape
    tm = min(ROUTER_TM, seq)
    tpb = seq // tm
    return pl.pallas_call(
        _router_kernel,
        grid=(m // tm,),
        in_specs=[pl.BlockSpec((tm, d), lambda i: (i, 0)),
                  pl.BlockSpec((None, 6, d), lambda i: (i // tpb, 0, 0)),
                  pl.BlockSpec((d, LANES), lambda i: (0, 0)),
                  pl.BlockSpec((1, LANES), lambda i: (0, 0))],
        out_specs=[pl.BlockSpec((tm, d), lambda i: (i, 0)),
                   pl.BlockSpec((tm, LANES), lambda i: (i, 0)),
                   pl.BlockSpec((SUBLANES, tm), lambda i: (0, i))],
        out_shape=[jax.ShapeDtypeStruct((m, d), F32),
                   jax.ShapeDtypeStruct((m, LANES), F32),
                   jax.ShapeDtypeStruct((SUBLANES, m), F32)],
        compiler_params=_cparams(("parallel",)),
        name="moe_router",
    )(x2, mod, rw, rb)


def _moe_meta_kernel(sel_ref, tri_ref, pos_ref, meta_ref, run1, run2, off1, off2, *, tm):
    p = pl.program_id(0)
    i = pl.program_id(1)
    T = sel_ref.shape[1]
    sub = lax.broadcasted_iota(jnp.int32, (SUBLANES, T), 0).astype(F32)
    oh1 = jnp.where(sel_ref[2:3, :] == sub, 1.0, 0.0)
    oh2 = jnp.where(sel_ref[3:4, :] == sub, 1.0, 0.0)
    c1 = jnp.sum(oh1, axis=1, keepdims=True)
    c2 = jnp.sum(oh2, axis=1, keepdims=True)

    @pl.when((p == 0) & (i == 0))
    def _():
        run1[...] = jnp.zeros_like(run1)
        run2[...] = jnp.zeros_like(run2)

    @pl.when((p == 1) & (i == 0))
    def _():
        cnt1 = run1[...]
        tot = cnt1 + run2[...]
        padded = jnp.floor((tot + (tm - 1)) * (1.0 / tm)) * tm
        e_idx = lax.broadcasted_iota(jnp.int32, padded.shape, 0)
        off = jnp.zeros_like(padded)
        for e in range(N_EXPERTS - 1):
            off = off + jnp.where(e_idx > e, padded[e:e + 1, :], 0.0)
        off1[...] = off
        off2[...] = off + cnt1
        meta_ref[0:SUBLANES, :] = off
        meta_ref[SUBLANES:2 * SUBLANES, :] = padded
        run1[...] = jnp.zeros_like(run1)
        run2[...] = jnp.zeros_like(run2)

    @pl.when(p == 1)
    def _():
        tri = tri_ref[...]
        pre1 = jnp.dot(oh1.astype(BF16), tri, preferred_element_type=F32)
        pre2 = jnp.dot(oh2.astype(BF16), tri, preferred_element_type=F32)
        pos1 = jnp.sum(oh1 * (pre1 + (off1[:, 0:1] + run1[:, 0:1])), axis=0, keepdims=True)
        pos2 = jnp.sum(oh2 * (pre2 + (off2[:, 0:1] + run2[:, 0:1])), axis=0, keepdims=True)
        row = lax.broadcasted_iota(jnp.int32, (SUBLANES, T), 0)
        pos = jnp.where(row == 0, pos1, jnp.where(row == 1, pos2, 0.0))
        pos_ref[...] = pos.astype(jnp.int32)

    run1[...] += c1
    run2[...] += c2


def _moe_meta(info_t, tm):
    m = info_t.shape[1]
    T = min(MOE_META_T, m)
    s_i = lax.broadcasted_iota(jnp.int32, (T, T), 0)
    t_i = lax.broadcasted_iota(jnp.int32, (T, T), 1)
    tri = (s_i < t_i).astype(BF16)
    return pl.pallas_call(
        functools.partial(_moe_meta_kernel, tm=tm),
        grid=(2, m // T),
        in_specs=[pl.BlockSpec((SUBLANES, T), lambda p, i: (0, i)),
                  pl.BlockSpec((T, T), lambda p, i: (0, 0))],
        out_specs=[pl.BlockSpec((SUBLANES, T), lambda p, i: (0, i * p)),
                   pl.BlockSpec((2 * SUBLANES, LANES), lambda p, i: (0, 0))],
        out_shape=[jax.ShapeDtypeStruct((SUBLANES, m), jnp.int32),
                   jax.ShapeDtypeStruct((2 * SUBLANES, LANES), F32)],
        scratch_shapes=[pltpu.VMEM((SUBLANES, LANES), F32)] * 4,
        compiler_params=_cparams(("arbitrary", "arbitrary")),
        name="moe_meta",
    )(info_t, tri)


def _row_copy(src_ref, src_row, dst_ref, dst_row, sem):
    return pltpu.make_async_copy(src_ref.at[pl.ds(src_row, 1), :], dst_ref.at[pl.ds(dst_row, 1), :], sem)


def _moe_gather_kernel(pos1_ref, pos2_ref, h_ref, hg_in_ref, hg_ref, sem):
    del hg_in_ref
    T = pos1_ref.shape[0]
    base = pl.program_id(0) * T

    def body(j, carry):
        _row_copy(h_ref, base + j, hg_ref, pos1_ref[j], sem).start()
        _row_copy(h_ref, base + j, hg_ref, pos2_ref[j], sem).start()
        return carry

    lax.fori_loop(0, T, body, 0, unroll=8)
    pltpu.make_async_copy(h_ref.at[pl.ds(0, 2 * T), :], hg_ref.at[pl.ds(0, 2 * T), :], sem).wait()


def _moe_gather(pos1, pos2, h, n_rows):
    m, d = h.shape
    T = min(MOE_DMA_T, m)
    smem = lambda: pl.BlockSpec((T,), lambda i: (i,), memory_space=pltpu.SMEM)
    hbm = lambda: pl.BlockSpec(memory_space=pl.ANY)
    return pl.pallas_call(
        _moe_gather_kernel,
        grid=(m // T,),
        in_specs=[smem(), smem(), hbm(), hbm()],
        out_specs=hbm(),
        out_shape=jax.ShapeDtypeStruct((n_rows, d), F32),
        scratch_shapes=[pltpu.SemaphoreType.DMA(())],
        input_output_aliases={3: 0},
        compiler_params=_cparams(("arbitrary",)),
        name="moe_gather",
    )(pos1, pos2, h, jnp.zeros((n_rows, d), F32))


def _moe_group_kernel(eid_ref, nv_ref, hg_ref, w1_ref, w3_ref, w2_ref, o_ref, h_scr, acc_scr):
    del eid_ref
    r = pl.program_id(0)
    f = pl.program_id(1)

    @pl.when(r < nv_ref[0])
    def _():
        @pl.when(f == 0)
        def _():
            h_scr[...] = hg_ref[...].astype(BF16)
            acc_scr[...] = jnp.zeros_like(acc_scr)

        h = h_scr[...]
        a = jnp.dot(h, w1_ref[...], preferred_element_type=F32)
        b = jnp.dot(h, w3_ref[...], preferred_element_type=F32)
        g = (a * _sigmoid(a) * b).astype(BF16)
        acc_scr[...] += jnp.dot(g, w2_ref[...], preferred_element_type=F32)

        @pl.when(f == pl.num_programs(1) - 1)
        def _():
            o_ref[...] = acc_scr[...]

    @pl.when((r >= nv_ref[0]) & (f == 0))
    def _():
        o_ref[...] = jnp.zeros_like(o_ref)


def _moe_group(eid, nvalid, hg, w13, w2, tm):
    n_rows, d = hg.shape
    ne, dff, _ = w2.shape
    tf = FFN_TF
    nf = dff // tf
    row = lambda r, f, eid, nv: (jnp.minimum(r, nv[0] - 1), 0)
    fe = lambda r, f, nv: jnp.where(r < nv[0], f, nf - 1)
    grid_spec = pltpu.PrefetchScalarGridSpec(
        num_scalar_prefetch=2,
        grid=(n_rows // tm, nf),
        in_specs=[pl.BlockSpec((tm, d), row),
                  pl.BlockSpec((None, d, tf), lambda r, f, eid, nv: (eid[r], 0, fe(r, f, nv))),
                  pl.BlockSpec((None, d, tf), lambda r, f, eid, nv: (eid[r], 0, nf + fe(r, f, nv))),
                  pl.BlockSpec((None, tf, d), lambda r, f, eid, nv: (eid[r], fe(r, f, nv), 0))],
        out_specs=pl.BlockSpec((tm, d), lambda r, f, eid, nv: (r, 0)),
        scratch_shapes=[pltpu.VMEM((tm, d), BF16), pltpu.VMEM((tm, d), F32)])
    return pl.pallas_call(
        _moe_group_kernel,
        grid_spec=grid_spec,
        out_shape=jax.ShapeDtypeStruct((n_rows, d), F32),
        compiler_params=_cparams(("arbitrary", "arbitrary")),
        name="moe_experts",
    )(eid, nvalid, hg, w13, w13, w2)


def _moe_combine_kernel(pos1_ref, pos2_ref, x_ref, mod_ref, info_ref, fg_ref, ys_ref, o_ref, y1, y2, sem,
                        *, final):
    T = x_ref.shape[0]

    def body(j, carry):
        _row_copy(ys_ref, pos1_ref[j], y1, j, sem).start()
        _row_copy(ys_ref, pos2_ref[j], y2, j, sem).start()
        return carry

    lax.fori_loop(0, T, body, 0, unroll=8)
    pltpu.make_async_copy(ys_ref.at[pl.ds(0, T), :], y1, sem).wait()
    pltpu.make_async_copy(ys_ref.at[pl.ds(0, T), :], y2, sem).wait()
    info = info_ref[...]
    y = info[:, 0:1] * y1[...] + info[:, 1:2] * y2[...]
    out = x_ref[...] + mod_ref[5:6, :] * y
    o_ref[...] = _final_norm(out, fg_ref) if final else out


def _moe_combine(pos1, pos2, x2, mod, info, final_g, ys, final, seq):
    m, d = x2.shape
    T = min(MOE_DMA_T, seq)
    tpb = seq // T
    smem = lambda: pl.BlockSpec((T,), lambda i: (i,), memory_space=pltpu.SMEM)
    return pl.pallas_call(
        functools.partial(_moe_combine_kernel, final=final),
        grid=(m // T,),
        in_specs=[smem(), smem(),
                  pl.BlockSpec((T, d), lambda i: (i, 0)),
                  pl.BlockSpec((None, 6, d), lambda i: (i // tpb, 0, 0)),
                  pl.BlockSpec((T, LANES), lambda i: (i, 0)),
                  pl.BlockSpec((1, d), lambda i: (0, 0)),
                  pl.BlockSpec(memory_space=pl.ANY)],
        out_specs=pl.BlockSpec((T, d), lambda i: (i, 0)),
        out_shape=jax.ShapeDtypeStruct((m, d), F32),
        scratch_shapes=[pltpu.VMEM((T, d), F32), pltpu.VMEM((T, d), F32), pltpu.SemaphoreType.DMA(())],
        compiler_params=_cparams(("arbitrary",)),
        name="moe_combine",
    )(pos1, pos2, x2, mod, info, final_g, ys)


def _moe(x2, mod, rw, rb, w13, w2, final_g, final, seq):
    m, d = x2.shape
    ne = w2.shape[0]
    tm = min(FFN_TM, seq)
    n_rows = 2 * m + ne * tm
    h, info, info_t = _router(x2, mod, rw, rb, seq)
    pos, meta = _moe_meta(info_t, tm)
    ends = (meta[0:ne, 0] + meta[SUBLANES:SUBLANES + ne, 0]).astype(jnp.int32)
    nvalid = ends[ne - 1] // tm
    starts = jnp.minimum(jnp.arange(n_rows // tm, dtype=jnp.int32), nvalid - 1) * tm
    eid = jnp.sum((starts[:, None] >= ends[None, :]).astype(jnp.int32), axis=1)
    hg = _moe_gather(pos[0], pos[1], h, n_rows)
    ys = _moe_group(eid, nvalid.reshape(1), hg, w13, w2, tm)
    return _moe_combine(pos[0], pos[1], x2, mod, info, final_g, ys, final, seq)


def kernel(x, c, ada_w, ada_b, norm_g, ml_w_in, ml_gate_b, ml_head_g, ml_w_out, ffn_w13, ffn_w2,
           lru_w_in, lru_conv_w, lru_conv_b, lru_gate_w, lru_gate_b, lru_a_param, lru_w_out,
           moe_router_w, moe_router_b, moe_w13, moe_w2, final_norm_g):
    batch, seq, d = x.shape
    depth = ada_w.shape[0]
    m = batch * seq
    qkw = ML_HEADS * ML_DQK
    vw = ML_HEADS * ML_DV

    ones = jnp.ones((depth, 1, d), F32)
    zeros = jnp.zeros((depth, 1, d), F32)
    g0, g1 = norm_g[:, 0:1, :], norm_g[:, 1:2, :]
    mul = jnp.concatenate([ones, g0, ones, ones, g1, ones], axis=1).reshape(depth, 6, 1, d)
    add = jnp.concatenate([zeros, g0, zeros, zeros, g1, zeros], axis=1).reshape(depth, 6, 1, d)
    mods = _ada(c, ada_w, ada_b, mul, add)

    x2 = x.reshape(m, d)
    for i in range(depth):
        j = i // 2
        mod = mods[i]
        last = i == depth - 1
        if i % 2 == 0:
            w_in = ml_w_in[j]
            wq = w_in[:, :qkw].astype(BF16)
            wkt = w_in[:, qkw:2 * qkw].T.astype(BF16)
            wvo = w_in[:, 2 * qkw:2 * qkw + 2 * vw].astype(BF16)
            wg = jnp.pad(w_in[:, 2 * qkw + 2 * vw:], ((0, 0), (0, LANES - 2 * ML_HEADS))).astype(BF16)
            bg = jnp.pad(ml_gate_b[j], (0, LANES - 2 * ML_HEADS)).reshape(1, LANES)
            q, kt, vo, gates = _mlproj(x2, mod, wq, wkt, wvo, wg, bg, seq)
            cols = _gateprep(gates, min(ML_CHUNK, seq))
            a_rows = cols[:, :ML_HEADS].T
            x2 = _mlstm(x2, mod, q, kt, vo, a_rows, cols, ml_head_g[j].reshape(1, vw),
                        ml_w_out[j].astype(BF16), batch, seq)
            x2 = _ffn(x2, mod, ffn_w13[j].astype(BF16), ffn_w2[j].astype(BF16), seq)
        else:
            gx = _lruproj(x2, mod, lru_w_in[j].astype(BF16), seq)
            gw = lru_gate_w[j]
            gb = lru_gate_b[j]
            wgx = gw[:, :, :LRU_BW].astype(BF16)
            wga = gw[:, :, LRU_BW:].astype(BF16)
            bgx = gb[:, :LRU_BW].reshape(1, -1)
            bga = gb[:, LRU_BW:].reshape(1, -1)
            x2 = _lru(x2, mod, gx, lru_conv_w[j], lru_conv_b[j].reshape(1, -1), wgx, wga, bgx, bga,
                      lru_a_param[j].reshape(1, -1), lru_w_out[j].astype(BF16), batch, seq)
            rw = jnp.pad(moe_router_w[j], ((0, 0), (0, LANES - N_EXPERTS)))
            rb = jnp.pad(moe_router_b[j], (0, LANES - N_EXPERTS), constant_values=-jnp.inf).reshape(1, LANES)
            x2 = _moe(x2, mod, rw, rb, moe_w13[j].astype(BF16), moe_w2[j].astype(BF16),
                      final_norm_g.reshape(1, d), last, seq)
    return x2.reshape(batch, seq, d)
```

```python
import functools

import jax
import jax.numpy as jnp
from jax import lax
from jax.experimental import pallas as pl
from jax.experimental.pallas import tpu as pltpu

F32 = jnp.float32
BF16 = jnp.bfloat16
HIGHEST = lax.Precision.HIGHEST

RMS_EPS = 1e-6
ML_HEADS = 8
ML_DQK = 64
ML_DV = 128
LRU_HEADS = 4
LRU_BW = 256
CONV_W = 4
LRU_C = 8.0
N_EXPERTS = 8

LANES = 128
SUBLANES = 8
VMEM_LIMIT = 56 * 1024 * 1024

ML_CHUNK = 256
LRU_TILE = 256
PROJ_TM = 512
FFN_TM = 1024
FFN_TF = 512
ROUTER_TM = 1024
MOE_META_T = 512
MOE_DMA_T = 1024


def _cparams(sem):
    return pltpu.CompilerParams(dimension_semantics=sem, vmem_limit_bytes=VMEM_LIMIT)


def _norm_mod(x, mod_ref, row):
    r = lax.rsqrt(jnp.mean(x * x, axis=-1, keepdims=True) + RMS_EPS)
    return (x * r) * mod_ref[row + 1:row + 2, :] + mod_ref[row:row + 1, :]


def _sigmoid(x):
    return 1.0 / (1.0 + jnp.exp(-x))


def _log_sigmoid(x):
    return jnp.minimum(x, 0.0) - jnp.log1p(jnp.exp(-jnp.abs(x)))


def _ada_kernel(c_ref, w_ref, b_ref, mul_ref, add_ref, o_ref):
    c = c_ref[...]
    ca = (c * _sigmoid(c)).astype(BF16)
    mod = jnp.dot(ca, w_ref[...].astype(BF16), preferred_element_type=F32) + b_ref[...]
    o_ref[...] = mul_ref[...] * mod + add_ref[...]


def _ada(c, ada_w, ada_b, mul, add):
    depth, d, _ = ada_w.shape
    b = c.shape[0]
    row = pl.BlockSpec((None, None, 1, d), lambda i, j: (i, j, 0, 0))
    out = pl.pallas_call(
        _ada_kernel,
        grid=(depth, 6),
        in_specs=[pl.BlockSpec((b, d), lambda i, j: (0, 0)),
                  pl.BlockSpec((None, d, d), lambda i, j: (i, 0, j)),
                  row, row, row],
        out_specs=pl.BlockSpec((None, None, b, d), lambda i, j: (i, j, 0, 0)),
        out_shape=jax.ShapeDtypeStruct((depth, 6, b, d), F32),
        compiler_params=_cparams(("arbitrary", "arbitrary")),
        name="ada_mod",
    )(c, ada_w, ada_b.reshape(depth, 6, 1, d), mul, add)
    return out.transpose(0, 2, 1, 3)


def _mlproj_kernel(x_ref, mod_ref, wq_ref, wkt_ref, wvo_ref, wg_ref, bg_ref,
                   q_ref, kt_ref, vo_ref, g_ref):
    h = _norm_mod(x_ref[...], mod_ref, 0).astype(BF16)
    q = jnp.dot(h, wq_ref[...], preferred_element_type=F32)
    q_ref[...] = (q * (ML_DQK ** -0.5)).astype(BF16)
    kt = lax.dot_general(wkt_ref[...], h, (((1,), (1,)), ((), ())), preferred_element_type=F32)
    kt_ref[...] = kt.astype(BF16)
    vo_ref[...] = jnp.dot(h, wvo_ref[...], preferred_element_type=F32).astype(BF16)
    g_ref[...] = jnp.dot(h, wg_ref[...], preferred_element_type=F32) + bg_ref[...]


def _mlproj(x2, mod, wq, wkt, wvo, wg, bg, seq):
    m, d = x2.shape
    tm = min(PROJ_TM, seq)
    tpb = seq // tm
    nq, nvo = wq.shape[1], wvo.shape[1]
    full = lambda shape: pl.BlockSpec(shape, lambda i: (0,) * len(shape))
    return pl.pallas_call(
        _mlproj_kernel,
        grid=(m // tm,),
        in_specs=[pl.BlockSpec((tm, d), lambda i: (i, 0)),
                  pl.BlockSpec((None, 6, d), lambda i: (i // tpb, 0, 0)),
                  full(wq.shape), full(wkt.shape), full(wvo.shape), full(wg.shape), full(bg.shape)],
        out_specs=[pl.BlockSpec((tm, nq), lambda i: (i, 0)),
                   pl.BlockSpec((nq, tm), lambda i: (0, i)),
                   pl.BlockSpec((tm, nvo), lambda i: (i, 0)),
                   pl.BlockSpec((tm, LANES), lambda i: (i, 0))],
        out_shape=[jax.ShapeDtypeStruct((m, nq), BF16),
                   jax.ShapeDtypeStruct((nq, m), BF16),
                   jax.ShapeDtypeStruct((m, nvo), BF16),
                   jax.ShapeDtypeStruct((m, LANES), F32)],
        compiler_params=_cparams(("parallel",)),
        name="mlstm_proj",
    )(x2, mod, wq, wkt, wvo, wg, bg)


def _gateprep_kernel(g_ref, o_ref):
    g = g_ref[...]
    n = g.shape[0]
    lf = _log_sigmoid(g)
    r = lax.broadcasted_iota(jnp.int32, (n, n), 0)
    s = lax.broadcasted_iota(jnp.int32, (n, n), 1)
    tril = jnp.where(s <= r, 1.0, 0.0).astype(F32)
    b = jnp.dot(tril, lf, precision=HIGHEST, preferred_element_type=F32)
    b_i = pltpu.roll(b, LANES - ML_HEADS, axis=1)
    lane = lax.broadcasted_iota(jnp.int32, g.shape, 1)
    o_ref[...] = jnp.where(lane < ML_HEADS, g - b_i, b)


def _gateprep(gates, chunk):
    m = gates.shape[0]
    return pl.pallas_call(
        _gateprep_kernel,
        grid=(m // chunk,),
        in_specs=[pl.BlockSpec((chunk, LANES), lambda i: (i, 0))],
        out_specs=pl.BlockSpec((chunk, LANES), lambda i: (i, 0)),
        out_shape=jax.ShapeDtypeStruct((m, LANES), F32),
        compiler_params=_cparams(("parallel",)),
        name="mlstm_gateprep",
    )(gates)


def _mlstm_kernel(x_ref, mod_ref, q_ref, kt_ref, vo_ref, arow_ref, col_ref, hg_ref, wout_ref,
                  o_ref, c_scr, m_scr, hs_scr):
    L = q_ref.shape[0]
    vw = ML_HEADS * ML_DV

    @pl.when(pl.program_id(1) == 0)
    def _():
        c_scr[...] = jnp.zeros_like(c_scr)
        m_scr[...] = jnp.zeros_like(m_scr)

    t_idx = lax.broadcasted_iota(jnp.int32, (L, L), 0)
    s_idx = lax.broadcasted_iota(jnp.int32, (L, L), 1)
    causal = s_idx <= t_idx
    lane_q = lax.broadcasted_iota(jnp.int32, (1, 2 * ML_DQK), 1)
    head_mask = [jnp.where(lane_q < ML_DQK, 1.0, 0.0).astype(BF16),
                 jnp.where(lane_q >= ML_DQK, 1.0, 0.0).astype(BF16)]
    lane_v = lax.broadcasted_iota(jnp.int32, (L, ML_DV), 1)
    ones_blk = jnp.where(lane_v == 0, 1.0, 0.0).astype(BF16)

    for h in range(ML_HEADS):
        p, half = h // 2, h % 2
        q2 = q_ref[:, p * 128:(p + 1) * 128]
        qm = q2 * head_mask[half]
        kt2 = kt_ref[p * 128:(p + 1) * 128, :]
        kt_h = kt_ref[h * ML_DQK:(h + 1) * ML_DQK, :]
        v = vo_ref[:, h * ML_DV:(h + 1) * ML_DV]
        og = vo_ref[:, vw + h * ML_DV: vw + (h + 1) * ML_DV]
        v_aug = jnp.concatenate([v, ones_blk], axis=1)
        a_row = arow_ref[h:h + 1, :]
        b_col = col_ref[:, ML_HEADS + h:ML_HEADS + h + 1]
        m_prev = m_scr[h]
        c_prev = c_scr[h]

        a_mat = jnp.where(causal, a_row, -jnp.inf)
        m_col = jnp.maximum(m_prev, jnp.max(a_mat, axis=1, keepdims=True))
        w = jnp.exp(a_mat - m_col)
        e_col = jnp.exp(m_prev - m_col)
        s = jnp.dot(qm, kt2, preferred_element_type=F32) * w
        c_pair = jnp.concatenate([c_prev, c_prev], axis=0).astype(BF16)
        inter = jnp.dot(qm, c_pair, preferred_element_type=F32)
        num_aug = jnp.dot(s.astype(BF16), v_aug, preferred_element_type=F32) + e_col * inter
        num = num_aug[:, :ML_DV]
        den = num_aug[:, ML_DV:ML_DV + 1]
        hh = num / jnp.maximum(jnp.abs(den), jnp.exp(-(b_col + m_col)))
        hn = hh * lax.rsqrt(jnp.mean(hh * hh, axis=-1, keepdims=True) + RMS_EPS)
        hn = hn * hg_ref[:, h * ML_DV:(h + 1) * ML_DV] * _sigmoid(og.astype(F32))
        hs_scr[:, h * ML_DV:(h + 1) * ML_DV] = hn.astype(BF16)

        m_last = m_col[L - 1:L, :]
        w_s = jnp.exp(a_row - m_last)
        decay = jnp.exp(m_prev - m_last)
        kw = (kt_h.astype(F32) * w_s).astype(BF16)
        c_scr[h] = decay * c_prev + jnp.dot(kw, v_aug, preferred_element_type=F32)
        m_scr[h] = b_col[L - 1:L, :] + m_last

    mix = jnp.dot(hs_scr[...], wout_ref[...], preferred_element_type=F32)
    o_ref[...] = x_ref[...] + mod_ref[2:3, :] * mix


def _mlstm(x2, mod, q, kt, vo, a_rows, cols, head_g, w_out, batch, seq):
    m, d = x2.shape
    L = min(ML_CHUNK, seq)
    nc = seq // L
    nq, nvo = q.shape[1], vo.shape[1]
    row_blk = lambda w: pl.BlockSpec((L, w), lambda b, c: (b * nc + c, 0))
    return pl.pallas_call(
        _mlstm_kernel,
        grid=(batch, nc),
        in_specs=[row_blk(d),
                  pl.BlockSpec((None, 6, d), lambda b, c: (b, 0, 0)),
                  row_blk(nq),
                  pl.BlockSpec((nq, L), lambda b, c: (0, b * nc + c)),
                  row_blk(nvo),
                  pl.BlockSpec((ML_HEADS, L), lambda b, c: (0, b * nc + c)),
                  row_blk(LANES),
                  pl.BlockSpec((1, d), lambda b, c: (0, 0)),
                  pl.BlockSpec((d, d), lambda b, c: (0, 0))],
        out_specs=row_blk(d),
        out_shape=jax.ShapeDtypeStruct((m, d), F32),
        scratch_shapes=[pltpu.VMEM((ML_HEADS, ML_DQK, 2 * ML_DV), F32),
                        pltpu.VMEM((ML_HEADS, 1, 1), F32),
                        pltpu.VMEM((L, d), BF16)],
        compiler_params=_cparams(("parallel", "arbitrary")),
        name="mlstm_core",
    )(x2, mod, q, kt, vo, a_rows, cols, head_g, w_out)


def _lru_kernel(x_ref, mod_ref, gx_ref, cw_ref, cb_ref, wgx_ref, wga_ref, bgx_ref, bga_ref, ap_ref,
                wout_ref, o_ref, xbuf, a_scr, u_scr, h_scr, hc_scr):
    T, W = h_scr.shape
    PAD = SUBLANES

    @pl.when(pl.program_id(1) == 0)
    def _():
        xbuf[0:PAD, :] = jnp.zeros((PAD, W), F32)
        hc_scr[...] = jnp.zeros_like(hc_scr)

    xbuf[PAD:PAD + T, :] = gx_ref[:, W:2 * W].astype(F32)
    xc = cb_ref[...] + xbuf[PAD - 3:PAD - 3 + T, :] * cw_ref[0:1, :]
    for j in range(1, CONV_W):
        xc = xc + xbuf[PAD - 3 + j:PAD - 3 + j + T, :] * cw_ref[j:j + 1, :]
    xbuf[0:PAD, :] = xbuf[T:T + PAD, :]

    xcb = xc.astype(BF16)
    sp = ap_ref[...]
    sp = jnp.maximum(-sp, 0.0) + jnp.log1p(jnp.exp(-jnp.abs(sp)))
    for hd in range(LRU_HEADS):
        sl = slice(hd * LRU_BW, (hd + 1) * LRU_BW)
        xh = xcb[:, sl]
        gxh = _sigmoid(jnp.dot(xh, wgx_ref[hd], preferred_element_type=F32) + bgx_ref[:, sl])
        gah = _sigmoid(jnp.dot(xh, wga_ref[hd], preferred_element_type=F32) + bga_ref[:, sl])
        log_a = (-LRU_C) * gah * sp[:, sl]
        a = jnp.exp(log_a)
        a_scr[:, sl] = a
        u_scr[:, sl] = xc[:, sl] * gxh * jnp.sqrt(jnp.tanh(-log_a) * (1.0 + a * a))

    row = lax.broadcasted_iota(jnp.int32, (SUBLANES, W), 0)

    def body(r, hc):
        base = pl.multiple_of(r * SUBLANES, SUBLANES)
        a8 = a_scr[pl.ds(base, SUBLANES), :]
        u8 = u_scr[pl.ds(base, SUBLANES), :]
        for sft in (1, 2, 4):
            a_sh = pltpu.roll(a8, sft, axis=0)
            u_sh = pltpu.roll(u8, sft, axis=0)
            valid = row >= sft
            u8 = jnp.where(valid, a8 * u_sh + u8, u8)
            a8 = jnp.where(valid, a8 * a_sh, a8)
        h8 = u8 + a8 * hc
        h_scr[pl.ds(base, SUBLANES), :] = h8
        return h8[SUBLANES - 1:SUBLANES, :]

    hc_scr[...] = lax.fori_loop(0, T // SUBLANES, body, hc_scr[...], unroll=2)

    gb = gx_ref[:, 0:W].astype(F32)
    y = (h_scr[...] * jax.nn.gelu(gb)).astype(BF16)
    mix = jnp.dot(y, wout_ref[...], preferred_element_type=F32)
    o_ref[...] = x_ref[...] + mod_ref[2:3, :] * mix


def _lru(x2, mod, gx, conv_w, conv_b, wgx, wga, bgx, bga, a_param, w_out, batch, seq):
    m, d = x2.shape
    w = w_out.shape[0]
    T = min(LRU_TILE, seq)
    nt = seq // T
    row_blk = lambda wd: pl.BlockSpec((T, wd), lambda b, t: (b * nt + t, 0))
    full = lambda a: pl.BlockSpec(a.shape, lambda b, t: (0,) * a.ndim)
    return pl.pallas_call(
        _lru_kernel,
        grid=(batch, nt),
        in_specs=[row_blk(d),
                  pl.BlockSpec((None, 6, d), lambda b, t: (b, 0, 0)),
                  row_blk(2 * w),
                  full(conv_w), full(conv_b), full(wgx), full(wga), full(bgx), full(bga), full(a_param),
                  full(w_out)],
        out_specs=row_blk(d),
        out_shape=jax.ShapeDtypeStruct((m, d), F32),
        scratch_shapes=[pltpu.VMEM((T + SUBLANES, w), F32),
                        pltpu.VMEM((T, w), F32),
                        pltpu.VMEM((T, w), F32),
                        pltpu.VMEM((T, w), F32),
                        pltpu.VMEM((1, w), F32)],
        compiler_params=_cparams(("parallel", "arbitrary")),
        name="rglru_core",
    )(x2, mod, gx, conv_w, conv_b, wgx, wga, bgx, bga, a_param, w_out)


def _lruproj_kernel(x_ref, mod_ref, w_ref, o_ref):
    h = _norm_mod(x_ref[...], mod_ref, 0).astype(BF16)
    o_ref[...] = jnp.dot(h, w_ref[...], preferred_element_type=F32).astype(BF16)


def _lruproj(x2, mod, w, seq):
    m, d = x2.shape
    n = w.shape[1]
    tm = min(PROJ_TM, seq)
    tpb = seq // tm
    return pl.pallas_call(
        _lruproj_kernel,
        grid=(m // tm,),
        in_specs=[pl.BlockSpec((tm, d), lambda i: (i, 0)),
                  pl.BlockSpec((None, 6, d), lambda i: (i // tpb, 0, 0)),
                  pl.BlockSpec((d, n), lambda i: (0, 0))],
        out_specs=pl.BlockSpec((tm, n), lambda i: (i, 0)),
        out_shape=jax.ShapeDtypeStruct((m, n), BF16),
        compiler_params=_cparams(("parallel",)),
        name="rglru_proj",
    )(x2, mod, w)


def _final_norm(x, g_ref):
    return x * lax.rsqrt(jnp.mean(x * x, axis=-1, keepdims=True) + RMS_EPS) * g_ref[...]


def _ffn_kernel(x_ref, mod_ref, w1_ref, w3_ref, w2_ref, o_ref, h_scr, acc_scr):
    f = pl.program_id(1)

    @pl.when(f == 0)
    def _():
        h_scr[...] = _norm_mod(x_ref[...], mod_ref, 3).astype(BF16)
        acc_scr[...] = jnp.zeros_like(acc_scr)

    h = h_scr[...]
    a = jnp.dot(h, w1_ref[...], preferred_element_type=F32)
    b = jnp.dot(h, w3_ref[...], preferred_element_type=F32)
    g = (a * _sigmoid(a) * b).astype(BF16)
    acc_scr[...] += jnp.dot(g, w2_ref[...], preferred_element_type=F32)

    @pl.when(f == pl.num_programs(1) - 1)
    def _():
        o_ref[...] = x_ref[...] + mod_ref[5:6, :] * acc_scr[...]


def _ffn(x2, mod, w13, w2, layer, seq):
    m, d = x2.shape
    dff = w2.shape[1]
    tm = min(FFN_TM, seq)
    tf = FFN_TF
    tpb = seq // tm
    nf = dff // tf
    return pl.pallas_call(
        _ffn_kernel,
        grid=(m // tm, nf),
        in_specs=[pl.BlockSpec((tm, d), lambda i, f: (i, 0)),
                  pl.BlockSpec((None, 6, d), lambda i, f: (i // tpb, 0, 0)),
                  pl.BlockSpec((None, d, tf), lambda i, f: (layer, 0, f)),
                  pl.BlockSpec((None, d, tf), lambda i, f: (layer, 0, nf + f)),
                  pl.BlockSpec((None, tf, d), lambda i, f: (layer, f, 0))],
        out_specs=pl.BlockSpec((tm, d), lambda i, f: (i, 0)),
        out_shape=jax.ShapeDtypeStruct((m, d), F32),
        scratch_shapes=[pltpu.VMEM((tm, d), BF16), pltpu.VMEM((tm, d), F32)],
        compiler_params=_cparams(("parallel", "arbitrary")),
        name="ffn_swiglu",
    )(x2, mod, w13, w13, w2)


def _router_kernel(x_ref, mod_ref, rw_ref, rb_ref, info_ref, info_t_ref):
    h = _norm_mod(x_ref[...], mod_ref, 3)
    logits = jnp.dot(h, rw_ref[...], precision=HIGHEST, preferred_element_type=F32) + rb_ref[...]
    lane = lax.broadcasted_iota(jnp.int32, logits.shape, 1).astype(F32)
    v1 = jnp.max(logits, axis=1, keepdims=True)
    i1 = jnp.min(jnp.where(logits == v1, lane, float(LANES)), axis=1, keepdims=True)
    rest = jnp.where(lane == i1, -jnp.inf, logits)
    v2 = jnp.max(rest, axis=1, keepdims=True)
    i2 = jnp.min(jnp.where(rest == v2, lane, float(LANES)), axis=1, keepdims=True)
    e2 = jnp.exp(v2 - v1)
    w1 = 1.0 / (1.0 + e2)
    w2 = e2 / (1.0 + e2)
    info = jnp.where(lane == 0.0, w1, jnp.where(lane == 1.0, w2, jnp.where(lane == 2.0, i1,
                                                                           jnp.where(lane == 3.0, i2, 0.0))))
    info_ref[...] = info
    info_t_ref[...] = info.T[0:SUBLANES, :]


def _router(x2, mod, rw, rb, seq):
    m, d = x2.shape
    tm = min(ROUTER_TM, seq)
    tpb = seq // tm
    return pl.pallas_call(
        _router_kernel,
        grid=(m // tm,),
        in_specs=[pl.BlockSpec((tm, d), lambda i: (i, 0)),
                  pl.BlockSpec((None, 6, d), lambda i: (i // tpb, 0, 0)),
                  pl.BlockSpec((d, LANES), lambda i: (0, 0)),
                  pl.BlockSpec((1, LANES), lambda i: (0, 0))],
        out_specs=[pl.BlockSpec((tm, LANES), lambda i: (i, 0)),
                   pl.BlockSpec((SUBLANES, tm), lambda i: (0, i))],
        out_shape=[jax.ShapeDtypeStruct((m, LANES), F32),
                   jax.ShapeDtypeStruct((SUBLANES, m), F32)],
        compiler_params=_cparams(("parallel",)),
        name="moe_router",
    )(x2, mod, rw, rb)


def _moe_meta_kernel(sel_ref, tri_ref, pos_ref, meta_ref, run1, run2, off1, off2, *, tm):
    p = pl.program_id(0)
    i = pl.program_id(1)
    T = sel_ref.shape[1]
    sub = lax.broadcasted_iota(jnp.int32, (SUBLANES, T), 0).astype(F32)
    oh1 = jnp.where(sel_ref[2:3, :] == sub, 1.0, 0.0)
    oh2 = jnp.where(sel_ref[3:4, :] == sub, 1.0, 0.0)
    c1 = jnp.sum(oh1, axis=1, keepdims=True)
    c2 = jnp.sum(oh2, axis=1, keepdims=True)

    @pl.when((p == 0) & (i == 0))
    def _():
        run1[...] = jnp.zeros_like(run1)
        run2[...] = jnp.zeros_like(run2)

    @pl.when((p == 1) & (i == 0))
    def _():
        cnt1 = run1[...]
        tot = cnt1 + run2[...]
        padded = jnp.floor((tot + (tm - 1)) * (1.0 / tm)) * tm
        e_idx = lax.broadcasted_iota(jnp.int32, padded.shape, 0)
        off = jnp.zeros_like(padded)
        for e in range(N_EXPERTS - 1):
            off = off + jnp.where(e_idx > e, padded[e:e + 1, :], 0.0)
        off1[...] = off
        off2[...] = off + cnt1
        meta_ref[0:SUBLANES, :] = off
        meta_ref[SUBLANES:2 * SUBLANES, :] = padded
        run1[...] = jnp.zeros_like(run1)
        run2[...] = jnp.zeros_like(run2)

    @pl.when(p == 1)
    def _():
        tri = tri_ref[...]
        pre1 = jnp.dot(oh1.astype(BF16), tri, preferred_element_type=F32)
        pre2 = jnp.dot(oh2.astype(BF16), tri, preferred_element_type=F32)
        pos1 = jnp.sum(oh1 * (pre1 + (off1[:, 0:1] + run1[:, 0:1])), axis=0, keepdims=True)
        pos2 = jnp.sum(oh2 * (pre2 + (off2[:, 0:1] + run2[:, 0:1])), axis=0, keepdims=True)
        row = lax.broadcasted_iota(jnp.int32, (SUBLANES, T), 0)
        pos = jnp.where(row == 0, pos1, jnp.where(row == 1, pos2, 0.0))
        pos_ref[...] = pos.astype(jnp.int32)

    run1[...] += c1
    run2[...] += c2


def _moe_meta(info_t, tm):
    m = info_t.shape[1]
    T = min(MOE_META_T, m)
    s_i = lax.broadcasted_iota(jnp.int32, (T, T), 0)
    t_i = lax.broadcasted_iota(jnp.int32, (T, T), 1)
    tri = (s_i < t_i).astype(BF16)
    return pl.pallas_call(
        functools.partial(_moe_meta_kernel, tm=tm),
        grid=(2, m // T),
        in_specs=[pl.BlockSpec((SUBLANES, T), lambda p, i: (0, i)),
                  pl.BlockSpec((T, T), lambda p, i: (0, 0))],
        out_specs=[pl.BlockSpec((SUBLANES, T), lambda p, i: (0, i * p)),
                   pl.BlockSpec((2 * SUBLANES, LANES), lambda p, i: (0, 0))],
        out_shape=[jax.ShapeDtypeStruct((SUBLANES, m), jnp.int32),
                   jax.ShapeDtypeStruct((2 * SUBLANES, LANES), F32)],
        scratch_shapes=[pltpu.VMEM((SUBLANES, LANES), F32)] * 4,
        compiler_params=_cparams(("arbitrary", "arbitrary")),
        name="moe_meta",
    )(info_t, tri)


def _row_copy(src_ref, src_row, dst_ref, dst_row, sem):
    return pltpu.make_async_copy(src_ref.at[pl.ds(src_row, 1), :], dst_ref.at[pl.ds(dst_row, 1), :], sem)


def _moe_dispatch_kernel(pos1_ref, pos2_ref, x_ref, mod_ref, hg_in_ref, hg_ref, h_scr, sem):
    del hg_in_ref
    T = x_ref.shape[0]
    h_scr[...] = _norm_mod(x_ref[...], mod_ref, 3)

    def body(j, carry):
        _row_copy(h_scr, j, hg_ref, pos1_ref[j], sem).start()
        _row_copy(h_scr, j, hg_ref, pos2_ref[j], sem).start()
        return carry

    lax.fori_loop(0, T, body, 0, unroll=8)
    pltpu.make_async_copy(h_scr, hg_ref.at[pl.ds(0, T), :], sem).wait()
    pltpu.make_async_copy(h_scr, hg_ref.at[pl.ds(0, T), :], sem).wait()


def _moe_dispatch(pos1, pos2, x2, mod, n_rows, seq):
    m, d = x2.shape
    T = min(MOE_DMA_T, seq)
    tpb = seq // T
    smem = lambda: pl.BlockSpec((T,), lambda i: (i,), memory_space=pltpu.SMEM)
    hbm = lambda: pl.BlockSpec(memory_space=pl.ANY)
    return pl.pallas_call(
        _moe_dispatch_kernel,
        grid=(m // T,),
        in_specs=[smem(), smem(),
                  pl.BlockSpec((T, d), lambda i: (i, 0)),
                  pl.BlockSpec((None, 6, d), lambda i: (i // tpb, 0, 0)),
                  hbm()],
        out_specs=hbm(),
        out_shape=jax.ShapeDtypeStruct((n_rows, d), F32),
        scratch_shapes=[pltpu.VMEM((T, d), F32), pltpu.SemaphoreType.DMA(())],
        input_output_aliases={4: 0},
        compiler_params=_cparams(("arbitrary",)),
        name="moe_dispatch",
    )(pos1, pos2, x2, mod, jnp.zeros((n_rows, d), F32))


def _moe_group_kernel(eid_ref, nv_ref, hg_ref, w1_ref, w3_ref, w2_ref, o_ref, h_scr, acc_scr):
    del eid_ref
    r = pl.program_id(0)
    f = pl.program_id(1)

    @pl.when(r < nv_ref[0])
    def _():
        @pl.when(f == 0)
        def _():
            h_scr[...] = hg_ref[...].astype(BF16)
            acc_scr[...] = jnp.zeros_like(acc_scr)

        h = h_scr[...]
        a = jnp.dot(h, w1_ref[...], preferred_element_type=F32)
        b = jnp.dot(h, w3_ref[...], preferred_element_type=F32)
        g = (a * _sigmoid(a) * b).astype(BF16)
        acc_scr[...] += jnp.dot(g, w2_ref[...], preferred_element_type=F32)

        @pl.when(f == pl.num_programs(1) - 1)
        def _():
            o_ref[...] = acc_scr[...]

    @pl.when((r >= nv_ref[0]) & (f == 0))
    def _():
        o_ref[...] = jnp.zeros_like(o_ref)


def _moe_group(eid, nvalid, hg, w13, w2, layer, tm):
    n_rows, d = hg.shape
    dff = w2.shape[2]
    tf = FFN_TF
    nf = dff // tf
    row = lambda r, f, eid, nv: (jnp.minimum(r, nv[0] - 1), 0)
    fe = lambda r, f, nv: jnp.where(r < nv[0], f, nf - 1)
    grid_spec = pltpu.PrefetchScalarGridSpec(
        num_scalar_prefetch=2,
        grid=(n_rows // tm, nf),
        in_specs=[pl.BlockSpec((tm, d), row),
                  pl.BlockSpec((None, None, d, tf), lambda r, f, eid, nv: (layer, eid[r], 0, fe(r, f, nv))),
                  pl.BlockSpec((None, None, d, tf),
                               lambda r, f, eid, nv: (layer, eid[r], 0, nf + fe(r, f, nv))),
                  pl.BlockSpec((None, None, tf, d), lambda r, f, eid, nv: (layer, eid[r], fe(r, f, nv), 0))],
        out_specs=pl.BlockSpec((tm, d), lambda r, f, eid, nv: (r, 0)),
        scratch_shapes=[pltpu.VMEM((tm, d), BF16), pltpu.VMEM((tm, d), F32)])
    return pl.pallas_call(
        _moe_group_kernel,
        grid_spec=grid_spec,
        out_shape=jax.ShapeDtypeStruct((n_rows, d), F32),
        compiler_params=_cparams(("arbitrary", "arbitrary")),
        name="moe_experts",
    )(eid, nvalid, hg, w13, w13, w2)


def _moe_combine_kernel(pos1_ref, pos2_ref, x_ref, mod_ref, info_ref, fg_ref, ys_ref, o_ref, y1, y2, sem,
                        *, final):
    T = x_ref.shape[0]

    def body(j, carry):
        _row_copy(ys_ref, pos1_ref[j], y1, j, sem).start()
        _row_copy(ys_ref, pos2_ref[j], y2, j, sem).start()
        return carry

    lax.fori_loop(0, T, body, 0, unroll=8)
    pltpu.make_async_copy(ys_ref.at[pl.ds(0, T), :], y1, sem).wait()
    pltpu.make_async_copy(ys_ref.at[pl.ds(0, T), :], y2, sem).wait()
    info = info_ref[...]
    y = info[:, 0:1] * y1[...] + info[:, 1:2] * y2[...]
    out = x_ref[...] + mod_ref[5:6, :] * y
    o_ref[...] = _final_norm(out, fg_ref) if final else out


def _moe_combine(pos1, pos2, x2, mod, info, final_g, ys, final, seq):
    m, d = x2.shape
    T = min(MOE_DMA_T, seq)
    tpb = seq // T
    smem = lambda: pl.BlockSpec((T,), lambda i: (i,), memory_space=pltpu.SMEM)
    return pl.pallas_call(
        functools.partial(_moe_combine_kernel, final=final),
        grid=(m // T,),
        in_specs=[smem(), smem(),
                  pl.BlockSpec((T, d), lambda i: (i, 0)),
                  pl.BlockSpec((None, 6, d), lambda i: (i // tpb, 0, 0)),
                  pl.BlockSpec((T, LANES), lambda i: (i, 0)),
                  pl.BlockSpec((1, d), lambda i: (0, 0)),
                  pl.BlockSpec(memory_space=pl.ANY)],
        out_specs=pl.BlockSpec((T, d), lambda i: (i, 0)),
        out_shape=jax.ShapeDtypeStruct((m, d), F32),
        scratch_shapes=[pltpu.VMEM((T, d), F32), pltpu.VMEM((T, d), F32), pltpu.SemaphoreType.DMA(())],
        compiler_params=_cparams(("arbitrary",)),
        name="moe_combine",
    )(pos1, pos2, x2, mod, info, final_g, ys)


def _moe(x2, mod, rw, rb, w13, w2, layer, final_g, final, seq):
    m, d = x2.shape
    ne = w2.shape[1]
    tm = min(FFN_TM, seq)
    n_rows = 2 * m + ne * tm
    info, info_t = _router(x2, mod, rw, rb, seq)
    pos, meta = _moe_meta(info_t, tm)
    ends = (meta[0:ne, 0] + meta[SUBLANES:SUBLANES + ne, 0]).astype(jnp.int32)
    nvalid = ends[ne - 1] // tm
    starts = jnp.minimum(jnp.arange(n_rows // tm, dtype=jnp.int32), nvalid - 1) * tm
    eid = jnp.sum((starts[:, None] >= ends[None, :]).astype(jnp.int32), axis=1)
    hg = _moe_dispatch(pos[0], pos[1], x2, mod, n_rows, seq)
    ys = _moe_group(eid, nvalid.reshape(1), hg, w13, w2, layer, tm)
    return _moe_combine(pos[0], pos[1], x2, mod, info, final_g, ys, final, seq)


def kernel(x, c, ada_w, ada_b, norm_g, ml_w_in, ml_gate_b, ml_head_g, ml_w_out, ffn_w13, ffn_w2,
           lru_w_in, lru_conv_w, lru_conv_b, lru_gate_w, lru_gate_b, lru_a_param, lru_w_out,
           moe_router_w, moe_router_b, moe_w13, moe_w2, final_norm_g):
    batch, seq, d = x.shape
    depth = ada_w.shape[0]
    m = batch * seq
    qkw = ML_HEADS * ML_DQK
    vw = ML_HEADS * ML_DV

    ones = jnp.ones((depth, 1, d), F32)
    zeros = jnp.zeros((depth, 1, d), F32)
    g0, g1 = norm_g[:, 0:1, :], norm_g[:, 1:2, :]
    mul = jnp.concatenate([ones, g0, ones, ones, g1, ones], axis=1).reshape(depth, 6, 1, d)
    add = jnp.concatenate([zeros, g0, zeros, zeros, g1, zeros], axis=1).reshape(depth, 6, 1, d)
    mods = _ada(c, ada_w, ada_b, mul, add)

    ffn_w13_b, ffn_w2_b = ffn_w13.astype(BF16), ffn_w2.astype(BF16)
    moe_w13_b, moe_w2_b = moe_w13.astype(BF16), moe_w2.astype(BF16)

    x2 = x.reshape(m, d)
    for i in range(depth):
        j = i // 2
        mod = mods[i]
        last = i == depth - 1
        if i % 2 == 0:
            w_in = ml_w_in[j]
            wq = w_in[:, :qkw].astype(BF16)
            wkt = w_in[:, qkw:2 * qkw].T.astype(BF16)
            wvo = w_in[:, 2 * qkw:2 * qkw + 2 * vw].astype(BF16)
            wg = jnp.pad(w_in[:, 2 * qkw + 2 * vw:], ((0, 0), (0, LANES - 2 * ML_HEADS))).astype(BF16)
            bg = jnp.pad(ml_gate_b[j], (0, LANES - 2 * ML_HEADS)).reshape(1, LANES)
            q, kt, vo, gates = _mlproj(x2, mod, wq, wkt, wvo, wg, bg, seq)
            cols = _gateprep(gates, min(ML_CHUNK, seq))
            a_rows = cols[:, :ML_HEADS].T
            x2 = _mlstm(x2, mod, q, kt, vo, a_rows, cols, ml_head_g[j].reshape(1, vw),
                        ml_w_out[j].astype(BF16), batch, seq)
            x2 = _ffn(x2, mod, ffn_w13_b, ffn_w2_b, j, seq)
        else:
            gx = _lruproj(x2, mod, lru_w_in[j].astype(BF16), seq)
            gw = lru_gate_w[j]
            gb = lru_gate_b[j]
            wgx = gw[:, :, :LRU_BW].astype(BF16)
            wga = gw[:, :, LRU_BW:].astype(BF16)
            bgx = gb[:, :LRU_BW].reshape(1, -1)
            bga = gb[:, LRU_BW:].reshape(1, -1)
            x2 = _lru(x2, mod, gx, lru_conv_w[j], lru_conv_b[j].reshape(1, -1), wgx, wga, bgx, bga,
                      lru_a_param[j].reshape(1, -1), lru_w_out[j].astype(BF16), batch, seq)
            rw = jnp.pad(moe_router_w[j], ((0, 0), (0, LANES - N_EXPERTS)))
            rb = jnp.pad(moe_router_b[j], (0, LANES - N_EXPERTS), constant_values=-jnp.inf).reshape(1, LANES)
            x2 = _moe(x2, mod, rw, rb, moe_w13_b, moe_w2_b, j, final_norm_g.reshape(1, d), last, seq)
    return x2.reshape(batch, seq, d)
```

```python
import functools

import jax
import jax.numpy as jnp
from jax import lax
from jax.experimental import pallas as pl
from jax.experimental.pallas import tpu as pltpu

F32 = jnp.float32
BF16 = jnp.bfloat16
HIGHEST = lax.Precision.HIGHEST

RMS_EPS = 1e-6
ML_HEADS = 8
ML_DQK = 64
ML_DV = 128
LRU_HEADS = 4
LRU_BW = 256
CONV_W = 4
LRU_C = 8.0
N_EXPERTS = 8

LANES = 128
SUBLANES = 8
VMEM_LIMIT = 56 * 1024 * 1024

ML_CHUNK = 256
LRU_TILE = 256
PROJ_TM = 512
FFN_TM = 1024
FFN_TF = 512
ROUTER_TM = 1024
GATEPREP_ROWS = 1024
MOE_META_T = 512
MOE_DMA_T = 1024


def _cparams(sem):
    return pltpu.CompilerParams(dimension_semantics=sem, vmem_limit_bytes=VMEM_LIMIT)


def _norm_mod(x, mod_ref, row):
    r = lax.rsqrt(jnp.mean(x * x, axis=-1, keepdims=True) + RMS_EPS)
    return (x * r) * mod_ref[row + 1:row + 2, :] + mod_ref[row:row + 1, :]


def _sigmoid(x):
    return 1.0 / (1.0 + jnp.exp(-x))


def _log_sigmoid(x):
    return jnp.minimum(x, 0.0) - jnp.log1p(jnp.exp(-jnp.abs(x)))


def _ada_kernel(c_ref, w_ref, b_ref, mul_ref, add_ref, o_ref):
    c = c_ref[...]
    ca = (c * _sigmoid(c)).astype(BF16)
    mod = jnp.dot(ca, w_ref[...].astype(BF16), preferred_element_type=F32) + b_ref[...]
    o_ref[...] = mul_ref[...] * mod + add_ref[...]


def _ada(c, ada_w, ada_b, mul, add):
    depth, d, _ = ada_w.shape
    b = c.shape[0]
    row = pl.BlockSpec((None, None, 1, d), lambda i, j: (i, j, 0, 0))
    out = pl.pallas_call(
        _ada_kernel,
        grid=(depth, 6),
        in_specs=[pl.BlockSpec((b, d), lambda i, j: (0, 0)),
                  pl.BlockSpec((None, d, d), lambda i, j: (i, 0, j)),
                  row, row, row],
        out_specs=pl.BlockSpec((None, None, b, d), lambda i, j: (i, j, 0, 0)),
        out_shape=jax.ShapeDtypeStruct((depth, 6, b, d), F32),
        compiler_params=_cparams(("arbitrary", "arbitrary")),
        name="ada_mod",
    )(c, ada_w, ada_b.reshape(depth, 6, 1, d), mul, add)
    return out.transpose(0, 2, 1, 3)


def _mlproj_kernel(x_ref, mod_ref, wq_ref, wkt_ref, wvo_ref, wg_ref, bg_ref,
                   q_ref, kt_ref, vo_ref, g_ref):
    h = _norm_mod(x_ref[...], mod_ref, 0).astype(BF16)
    q = jnp.dot(h, wq_ref[...], preferred_element_type=F32)
    q_ref[...] = (q * (ML_DQK ** -0.5)).astype(BF16)
    kt = lax.dot_general(wkt_ref[...], h, (((1,), (1,)), ((), ())), preferred_element_type=F32)
    kt_ref[...] = kt.astype(BF16)
    vo_ref[...] = jnp.dot(h, wvo_ref[...], preferred_element_type=F32).astype(BF16)
    g_ref[...] = jnp.dot(h, wg_ref[...], preferred_element_type=F32) + bg_ref[...]


def _mlproj(x2, mod, wq, wkt, wvo, wg, bg, seq):
    m, d = x2.shape
    tm = min(PROJ_TM, seq)
    tpb = seq // tm
    nq, nvo = wq.shape[1], wvo.shape[1]
    full = lambda shape: pl.BlockSpec(shape, lambda i: (0,) * len(shape))
    return pl.pallas_call(
        _mlproj_kernel,
        grid=(m // tm,),
        in_specs=[pl.BlockSpec((tm, d), lambda i: (i, 0)),
                  pl.BlockSpec((None, 6, d), lambda i: (i // tpb, 0, 0)),
                  full(wq.shape), full(wkt.shape), full(wvo.shape), full(wg.shape), full(bg.shape)],
        out_specs=[pl.BlockSpec((tm, nq), lambda i: (i, 0)),
                   pl.BlockSpec((nq, tm), lambda i: (0, i)),
                   pl.BlockSpec((tm, nvo), lambda i: (i, 0)),
                   pl.BlockSpec((tm, LANES), lambda i: (i, 0))],
        out_shape=[jax.ShapeDtypeStruct((m, nq), BF16),
                   jax.ShapeDtypeStruct((nq, m), BF16),
                   jax.ShapeDtypeStruct((m, nvo), BF16),
                   jax.ShapeDtypeStruct((m, LANES), F32)],
        compiler_params=_cparams(("parallel",)),
        name="mlstm_proj",
    )(x2, mod, wq, wkt, wvo, wg, bg)


def _gateprep_kernel(g_ref, o_ref, *, chunk):
    n = chunk
    r = lax.broadcasted_iota(jnp.int32, (n, n), 0)
    s = lax.broadcasted_iota(jnp.int32, (n, n), 1)
    tril = jnp.where(s <= r, 1.0, 0.0).astype(F32)
    lane = lax.broadcasted_iota(jnp.int32, (n, LANES), 1)
    for c in range(g_ref.shape[0] // chunk):
        g = g_ref[c * chunk:(c + 1) * chunk, :]
        b = jnp.dot(tril, _log_sigmoid(g), precision=HIGHEST, preferred_element_type=F32)
        b_i = pltpu.roll(b, LANES - ML_HEADS, axis=1)
        o_ref[c * chunk:(c + 1) * chunk, :] = jnp.where(lane < ML_HEADS, g - b_i, b)


def _gateprep(gates, chunk, seq):
    m = gates.shape[0]
    rows = min(GATEPREP_ROWS, seq)
    return pl.pallas_call(
        functools.partial(_gateprep_kernel, chunk=chunk),
        grid=(m // rows,),
        in_specs=[pl.BlockSpec((rows, LANES), lambda i: (i, 0))],
        out_specs=pl.BlockSpec((rows, LANES), lambda i: (i, 0)),
        out_shape=jax.ShapeDtypeStruct((m, LANES), F32),
        compiler_params=_cparams(("parallel",)),
        name="mlstm_gateprep",
    )(gates)


def _mlstm_kernel(x_ref, mod_ref, q_ref, kt_ref, vo_ref, arow_ref, col_ref, hg_ref, wout_ref,
                  o_ref, c_scr, m_scr, hs_scr):
    L = q_ref.shape[0]
    vw = ML_HEADS * ML_DV

    @pl.when(pl.program_id(1) == 0)
    def _():
        c_scr[...] = jnp.zeros_like(c_scr)
        m_scr[...] = jnp.zeros_like(m_scr)

    t_idx = lax.broadcasted_iota(jnp.int32, (L, L), 0)
    s_idx = lax.broadcasted_iota(jnp.int32, (L, L), 1)
    causal = s_idx <= t_idx
    lane_q = lax.broadcasted_iota(jnp.int32, (1, 2 * ML_DQK), 1)
    head_mask = [jnp.where(lane_q < ML_DQK, 1.0, 0.0).astype(BF16),
                 jnp.where(lane_q >= ML_DQK, 1.0, 0.0).astype(BF16)]
    lane_v = lax.broadcasted_iota(jnp.int32, (L, ML_DV), 1)
    ones_blk = jnp.ones((L, ML_DV), BF16)

    heads = range(ML_HEADS)
    m_all = [m_scr[h] for h in heads]
    c_all = [c_scr[h] for h in heads]
    a_rows = [arow_ref[h:h + 1, :] for h in heads]
    b_cols = [col_ref[:, ML_HEADS + h:ML_HEADS + h + 1] for h in heads]

    a_mats = [jnp.where(causal, a_rows[h], -jnp.inf) for h in heads]
    m_cols = [jnp.maximum(m_all[h], jnp.max(a_mats[h], axis=1, keepdims=True)) for h in heads]
    ws = [jnp.exp(a_mats[h] - m_cols[h]) for h in heads]
    e_cols = [jnp.exp(m_all[h] - m_cols[h]) for h in heads]
    floors = [jnp.exp(-(b_cols[h] + m_cols[h])) for h in heads]

    v_augs, ss, inters = [], [], []
    for h in heads:
        p, half = h // 2, h % 2
        qm = q_ref[:, p * 128:(p + 1) * 128] * head_mask[half]
        kt2 = kt_ref[p * 128:(p + 1) * 128, :]
        v_augs.append(jnp.concatenate([vo_ref[:, h * ML_DV:(h + 1) * ML_DV], ones_blk], axis=1))
        ss.append((jnp.dot(qm, kt2, preferred_element_type=F32) * ws[h]).astype(BF16))
        c_pair = jnp.concatenate([c_all[h], c_all[h]], axis=0).astype(BF16)
        inters.append(jnp.dot(qm, c_pair, preferred_element_type=F32))

    hhs = []
    for h in heads:
        num_aug = jnp.dot(ss[h], v_augs[h], preferred_element_type=F32) + e_cols[h] * inters[h]
        hhs.append(num_aug[:, :ML_DV] / jnp.maximum(jnp.abs(num_aug[:, ML_DV:2 * ML_DV]), floors[h]))

    rs = [lax.rsqrt(jnp.mean(hhs[h] * hhs[h], axis=-1, keepdims=True) + RMS_EPS) for h in heads]
    hs_all = []
    for h in heads:
        og = vo_ref[:, vw + h * ML_DV: vw + (h + 1) * ML_DV]
        hn = hhs[h] * rs[h] * hg_ref[:, h * ML_DV:(h + 1) * ML_DV] * _sigmoid(og.astype(F32))
        hs_all.append(hn.astype(BF16))

    c_new, m_new = [], []
    for h in heads:
        m_last = m_cols[h][L - 1:L, :]
        w_s = jnp.exp(a_rows[h] - m_last)
        decay = jnp.exp(m_all[h] - m_last)
        kw = (kt_ref[h * ML_DQK:(h + 1) * ML_DQK, :].astype(F32) * w_s).astype(BF16)
        c_new.append(decay * c_all[h] + jnp.dot(kw, v_augs[h], preferred_element_type=F32))
        m_new.append(b_cols[h][L - 1:L, :] + m_last)

    for h in range(ML_HEADS):
        c_scr[h] = c_new[h]
        m_scr[h] = m_new[h]
    mix = jnp.dot(jnp.concatenate(hs_all, axis=1), wout_ref[...], preferred_element_type=F32)
    o_ref[...] = x_ref[...] + mod_ref[2:3, :] * mix


def _mlstm(x2, mod, q, kt, vo, a_rows, cols, head_g, w_out, batch, seq):
    m, d = x2.shape
    L = min(ML_CHUNK, seq)
    nc = seq // L
    nq, nvo = q.shape[1], vo.shape[1]
    row_blk = lambda w: pl.BlockSpec((L, w), lambda b, c: (b * nc + c, 0))
    return pl.pallas_call(
        _mlstm_kernel,
        grid=(batch, nc),
        in_specs=[row_blk(d),
                  pl.BlockSpec((None, 6, d), lambda b, c: (b, 0, 0)),
                  row_blk(nq),
                  pl.BlockSpec((nq, L), lambda b, c: (0, b * nc + c)),
                  row_blk(nvo),
                  pl.BlockSpec((ML_HEADS, L), lambda b, c: (0, b * nc + c)),
                  row_blk(LANES),
                  pl.BlockSpec((1, d), lambda b, c: (0, 0)),
                  pl.BlockSpec((d, d), lambda b, c: (0, 0))],
        out_specs=row_blk(d),
        out_shape=jax.ShapeDtypeStruct((m, d), F32),
        scratch_shapes=[pltpu.VMEM((ML_HEADS, ML_DQK, 2 * ML_DV), F32),
                        pltpu.VMEM((ML_HEADS, 1, 1), F32),
                        pltpu.VMEM((L, d), BF16)],
        compiler_params=_cparams(("parallel", "arbitrary")),
        name="mlstm_core",
    )(x2, mod, q, kt, vo, a_rows, cols, head_g, w_out)


def _lru_kernel(x_ref, mod_ref, gx_ref, cw_ref, cb_ref, wgx_ref, wga_ref, bgx_ref, bga_ref, ap_ref,
                wout_ref, o_ref, xbuf, a_scr, u_scr, h_scr, hc_scr):
    T, W = h_scr.shape
    PAD = SUBLANES

    @pl.when(pl.program_id(1) == 0)
    def _():
        xbuf[0:PAD, :] = jnp.zeros((PAD, W), F32)
        hc_scr[...] = jnp.zeros_like(hc_scr)

    xbuf[PAD:PAD + T, :] = gx_ref[:, W:2 * W].astype(F32)
    xc = cb_ref[...] + xbuf[PAD - 3:PAD - 3 + T, :] * cw_ref[0:1, :]
    for j in range(1, CONV_W):
        xc = xc + xbuf[PAD - 3 + j:PAD - 3 + j + T, :] * cw_ref[j:j + 1, :]
    xbuf[0:PAD, :] = xbuf[T:T + PAD, :]

    xcb = xc.astype(BF16)
    sp = ap_ref[...]
    sp = jnp.maximum(-sp, 0.0) + jnp.log1p(jnp.exp(-jnp.abs(sp)))
    for hd in range(LRU_HEADS):
        sl = slice(hd * LRU_BW, (hd + 1) * LRU_BW)
        xh = xcb[:, sl]
        gxh = _sigmoid(jnp.dot(xh, wgx_ref[hd], preferred_element_type=F32) + bgx_ref[:, sl])
        gah = _sigmoid(jnp.dot(xh, wga_ref[hd], preferred_element_type=F32) + bga_ref[:, sl])
        log_a = (-LRU_C) * gah * sp[:, sl]
        a = jnp.exp(log_a)
        a_scr[:, sl] = a
        u_scr[:, sl] = xc[:, sl] * gxh * jnp.sqrt(jnp.tanh(-log_a) * (1.0 + a * a))

    row = lax.broadcasted_iota(jnp.int32, (SUBLANES, W), 0)

    first = row == 0

    def body(r, hc):
        base = pl.multiple_of(r * SUBLANES, SUBLANES)
        a8 = a_scr[pl.ds(base, SUBLANES), :]
        u8 = u_scr[pl.ds(base, SUBLANES), :]
        u8 = u8 + jnp.where(first, a8 * hc, 0.0)
        a8 = jnp.where(first, 0.0, a8)
        for sft in (1, 2, 4):
            u8 = a8 * pltpu.roll(u8, sft, axis=0) + u8
            if sft < 4:
                a8 = a8 * pltpu.roll(a8, sft, axis=0)
        h_scr[pl.ds(base, SUBLANES), :] = u8
        return u8[SUBLANES - 1:SUBLANES, :]

    hc_scr[...] = lax.fori_loop(0, T // SUBLANES, body, hc_scr[...], unroll=2)

    gb = gx_ref[:, 0:W].astype(F32)
    y = (h_scr[...] * jax.nn.gelu(gb)).astype(BF16)
    mix = jnp.dot(y, wout_ref[...], preferred_element_type=F32)
    o_ref[...] = x_ref[...] + mod_ref[2:3, :] * mix


def _lru(x2, mod, gx, conv_w, conv_b, wgx, wga, bgx, bga, a_param, w_out, batch, seq):
    m, d = x2.shape
    w = w_out.shape[0]
    T = min(LRU_TILE, seq)
    nt = seq // T
    row_blk = lambda wd: pl.BlockSpec((T, wd), lambda b, t: (b * nt + t, 0))
    full = lambda a: pl.BlockSpec(a.shape, lambda b, t: (0,) * a.ndim)
    return pl.pallas_call(
        _lru_kernel,
        grid=(batch, nt),
        in_specs=[row_blk(d),
                  pl.BlockSpec((None, 6, d), lambda b, t: (b, 0, 0)),
                  row_blk(2 * w),
                  full(conv_w), full(conv_b), full(wgx), full(wga), full(bgx), full(bga), full(a_param),
                  full(w_out)],
        out_specs=row_blk(d),
        out_shape=jax.ShapeDtypeStruct((m, d), F32),
        scratch_shapes=[pltpu.VMEM((T + SUBLANES, w), F32),
                        pltpu.VMEM((T, w), F32),
                        pltpu.VMEM((T, w), F32),
                        pltpu.VMEM((T, w), F32),
                        pltpu.VMEM((1, w), F32)],
        compiler_params=_cparams(("parallel", "arbitrary")),
        name="rglru_core",
    )(x2, mod, gx, conv_w, conv_b, wgx, wga, bgx, bga, a_param, w_out)


def _lruproj_kernel(x_ref, mod_ref, w_ref, o_ref):
    h = _norm_mod(x_ref[...], mod_ref, 0).astype(BF16)
    o_ref[...] = jnp.dot(h, w_ref[...], preferred_element_type=F32).astype(BF16)


def _lruproj(x2, mod, w, seq):
    m, d = x2.shape
    n = w.shape[1]
    tm = min(PROJ_TM, seq)
    tpb = seq // tm
    return pl.pallas_call(
        _lruproj_kernel,
        grid=(m // tm,),
        in_specs=[pl.BlockSpec((tm, d), lambda i: (i, 0)),
                  pl.BlockSpec((None, 6, d), lambda i: (i // tpb, 0, 0)),
                  pl.BlockSpec((d, n), lambda i: (0, 0))],
        out_specs=pl.BlockSpec((tm, n), lambda i: (i, 0)),
        out_shape=jax.ShapeDtypeStruct((m, n), BF16),
        compiler_params=_cparams(("parallel",)),
        name="rglru_proj",
    )(x2, mod, w)


def _final_norm(x, g_ref):
    return x * lax.rsqrt(jnp.mean(x * x, axis=-1, keepdims=True) + RMS_EPS) * g_ref[...]


def _ffn_kernel(x_ref, mod_ref, w1_ref, w3_ref, w2_ref, o_ref, h_scr, acc_scr):
    f = pl.program_id(1)

    @pl.when(f == 0)
    def _():
        h_scr[...] = _norm_mod(x_ref[...], mod_ref, 3).astype(BF16)
        acc_scr[...] = jnp.zeros_like(acc_scr)

    h = h_scr[...]
    a = jnp.dot(h, w1_ref[...], preferred_element_type=F32)
    b = jnp.dot(h, w3_ref[...], preferred_element_type=F32)
    g = (a * _sigmoid(a) * b).astype(BF16)
    acc_scr[...] += jnp.dot(g, w2_ref[...], preferred_element_type=F32)

    @pl.when(f == pl.num_programs(1) - 1)
    def _():
        o_ref[...] = x_ref[...] + mod_ref[5:6, :] * acc_scr[...]


def _ffn(x2, mod, w13, w2, layer, seq):
    m, d = x2.shape
    dff = w2.shape[1]
    tm = min(FFN_TM, seq)
    tf = FFN_TF
    tpb = seq // tm
    nf = dff // tf
    return pl.pallas_call(
        _ffn_kernel,
        grid=(m // tm, nf),
        in_specs=[pl.BlockSpec((tm, d), lambda i, f: (i, 0)),
                  pl.BlockSpec((None, 6, d), lambda i, f: (i // tpb, 0, 0)),
                  pl.BlockSpec((None, d, tf), lambda i, f: (layer, 0, f)),
                  pl.BlockSpec((None, d, tf), lambda i, f: (layer, 0, nf + f)),
                  pl.BlockSpec((None, tf, d), lambda i, f: (layer, f, 0))],
        out_specs=pl.BlockSpec((tm, d), lambda i, f: (i, 0)),
        out_shape=jax.ShapeDtypeStruct((m, d), F32),
        scratch_shapes=[pltpu.VMEM((tm, d), BF16), pltpu.VMEM((tm, d), F32)],
        compiler_params=_cparams(("parallel", "arbitrary")),
        name="ffn_swiglu",
    )(x2, mod, w13, w13, w2)


def _router_kernel(x_ref, mod_ref, rw_ref, rwh_ref, rb_ref, info_ref, info_t_ref):
    h = _norm_mod(x_ref[...], mod_ref, 3)
    h_hi = h.astype(BF16)
    h_lo = (h - h_hi.astype(F32)).astype(BF16)
    p_hi = jnp.dot(h_hi, rw_ref[...], preferred_element_type=F32)
    p_lo = jnp.dot(h_lo, rwh_ref[...], preferred_element_type=F32)
    logits = p_hi + pltpu.roll(p_hi, LANES - N_EXPERTS, axis=1) + p_lo + rb_ref[...]
    lt = logits.T[0:N_EXPERTS, :]
    tm = lt.shape[1]
    sub = lax.broadcasted_iota(jnp.int32, lt.shape, 0).astype(F32)
    v1 = jnp.max(lt, axis=0, keepdims=True)
    i1 = jnp.min(jnp.where(lt == v1, sub, float(N_EXPERTS)), axis=0, keepdims=True)
    rest = jnp.where(sub == i1, -jnp.inf, lt)
    v2 = jnp.max(rest, axis=0, keepdims=True)
    i2 = jnp.min(jnp.where(rest == v2, sub, float(N_EXPERTS)), axis=0, keepdims=True)
    e2 = jnp.exp(v2 - v1)
    w1 = 1.0 / (1.0 + e2)
    w2 = e2 / (1.0 + e2)
    info_t = jnp.where(sub == 0.0, w1, jnp.where(sub == 1.0, w2, jnp.where(sub == 2.0, i1,
                                                                          jnp.where(sub == 3.0, i2, 0.0))))
    info_t_ref[...] = info_t
    info_ref[...] = jnp.concatenate([info_t, jnp.zeros((LANES - N_EXPERTS, tm), F32)], axis=0).T


def _router(x2, mod, router_w, router_b, seq):
    m, d = x2.shape
    tm = min(ROUTER_TM, seq)
    tpb = seq // tm
    ne = router_w.shape[1]
    w_hi = router_w.astype(BF16)
    w_lo = (router_w - w_hi.astype(F32)).astype(BF16)
    rw = jnp.pad(jnp.concatenate([w_hi, w_lo], axis=1), ((0, 0), (0, LANES - 2 * ne)))
    rwh = jnp.pad(w_hi, ((0, 0), (0, LANES - ne)))
    rb = jnp.pad(router_b, (0, LANES - ne), constant_values=-jnp.inf).reshape(1, LANES)
    return pl.pallas_call(
        _router_kernel,
        grid=(m // tm,),
        in_specs=[pl.BlockSpec((tm, d), lambda i: (i, 0)),
                  pl.BlockSpec((None, 6, d), lambda i: (i // tpb, 0, 0)),
                  pl.BlockSpec((d, LANES), lambda i: (0, 0)),
                  pl.BlockSpec((d, LANES), lambda i: (0, 0)),
                  pl.BlockSpec((1, LANES), lambda i: (0, 0))],
        out_specs=[pl.BlockSpec((tm, LANES), lambda i: (i, 0)),
                   pl.BlockSpec((SUBLANES, tm), lambda i: (0, i))],
        out_shape=[jax.ShapeDtypeStruct((m, LANES), F32),
                   jax.ShapeDtypeStruct((SUBLANES, m), F32)],
        compiler_params=_cparams(("parallel",)),
        name="moe_router",
    )(x2, mod, rw, rwh, rb)


def _moe_meta_kernel(sel_ref, tri_ref, pos_ref, meta_ref, run1, run2, off1, off2, *, tm):
    p = pl.program_id(0)
    i = pl.program_id(1)
    T = sel_ref.shape[1]
    sub = lax.broadcasted_iota(jnp.int32, (SUBLANES, T), 0).astype(F32)
    oh1 = jnp.where(sel_ref[2:3, :] == sub, 1.0, 0.0)
    oh2 = jnp.where(sel_ref[3:4, :] == sub, 1.0, 0.0)
    c1 = jnp.sum(oh1, axis=1, keepdims=True)
    c2 = jnp.sum(oh2, axis=1, keepdims=True)

    @pl.when((p == 0) & (i == 0))
    def _():
        run1[...] = jnp.zeros_like(run1)
        run2[...] = jnp.zeros_like(run2)

    @pl.when((p == 1) & (i == 0))
    def _():
        cnt1 = run1[...]
        tot = cnt1 + run2[...]
        padded = jnp.floor((tot + (tm - 1)) * (1.0 / tm)) * tm
        e_idx = lax.broadcasted_iota(jnp.int32, padded.shape, 0)
        off = jnp.zeros_like(padded)
        for e in range(N_EXPERTS - 1):
            off = off + jnp.where(e_idx > e, padded[e:e + 1, :], 0.0)
        off1[...] = off
        off2[...] = off + cnt1
        meta_ref[0:SUBLANES, :] = off
        meta_ref[SUBLANES:2 * SUBLANES, :] = padded
        meta_ref[2 * SUBLANES:3 * SUBLANES, :] = tot
        run1[...] = jnp.zeros_like(run1)
        run2[...] = jnp.zeros_like(run2)

    @pl.when(p == 1)
    def _():
        tri = tri_ref[...]
        pre1 = jnp.dot(oh1.astype(BF16), tri, preferred_element_type=F32)
        pre2 = jnp.dot(oh2.astype(BF16), tri, preferred_element_type=F32)
        pos1 = jnp.sum(oh1 * (pre1 + (off1[:, 0:1] + run1[:, 0:1])), axis=0, keepdims=True)
        pos2 = jnp.sum(oh2 * (pre2 + (off2[:, 0:1] + run2[:, 0:1])), axis=0, keepdims=True)
        row = lax.broadcasted_iota(jnp.int32, (SUBLANES, T), 0)
        pos = jnp.where(row == 0, pos1, jnp.where(row == 1, pos2, 0.0))
        pos_ref[...] = pos.astype(jnp.int32)

    run1[...] += c1
    run2[...] += c2


def _moe_meta(info_t, tm):
    m = info_t.shape[1]
    T = min(MOE_META_T, m)
    s_i = lax.broadcasted_iota(jnp.int32, (T, T), 0)
    t_i = lax.broadcasted_iota(jnp.int32, (T, T), 1)
    tri = (s_i < t_i).astype(BF16)
    return pl.pallas_call(
        functools.partial(_moe_meta_kernel, tm=tm),
        grid=(2, m // T),
        in_specs=[pl.BlockSpec((SUBLANES, T), lambda p, i: (0, i)),
                  pl.BlockSpec((T, T), lambda p, i: (0, 0))],
        out_specs=[pl.BlockSpec((SUBLANES, T), lambda p, i: (0, i * p)),
                   pl.BlockSpec((3 * SUBLANES, LANES), lambda p, i: (0, 0))],
        out_shape=[jax.ShapeDtypeStruct((SUBLANES, m), jnp.int32),
                   jax.ShapeDtypeStruct((3 * SUBLANES, LANES), F32)],
        scratch_shapes=[pltpu.VMEM((SUBLANES, LANES), F32)] * 4,
        compiler_params=_cparams(("arbitrary", "arbitrary")),
        name="moe_meta",
    )(info_t, tri)


def _row_copy(src_ref, src_row, dst_ref, dst_row, sem):
    return pltpu.make_async_copy(src_ref.at[pl.ds(src_row, 1), :], dst_ref.at[pl.ds(dst_row, 1), :], sem)


def _moe_dispatch_kernel(fill_ref, pos1_ref, pos2_ref, x_ref, mod_ref, hg_ref, h_scr, z_scr, sem, zsem, *, tm):
    T = x_ref.shape[0]
    ne = N_EXPERTS

    def zero_row(e, r):
        return _row_copy(z_scr, 0, hg_ref, fill_ref[e] + r, zsem)

    def zero_tile(t):
        row0 = pl.multiple_of((fill_ref[2 * ne] + t) * tm, tm)
        return pltpu.make_async_copy(z_scr, hg_ref.at[pl.ds(row0, tm), :], zsem)

    def for_each_fill(fn):
        for e in range(ne):
            lax.fori_loop(0, fill_ref[ne + e], lambda r, c, e=e: (fn(zero_row(e, r)), c)[1], 0)
        lax.fori_loop(0, fill_ref[2 * ne + 1], lambda t, c: (fn(zero_tile(t)), c)[1], 0)

    @pl.when(pl.program_id(0) == 0)
    def _():
        z_scr[...] = jnp.zeros_like(z_scr)
        for_each_fill(lambda cp: cp.start())
        for_each_fill(lambda cp: cp.wait())

    h_scr[...] = _norm_mod(x_ref[...], mod_ref, 3)

    def body(j, carry):
        _row_copy(h_scr, j, hg_ref, pos1_ref[j], sem).start()
        _row_copy(h_scr, j, hg_ref, pos2_ref[j], sem).start()
        return carry

    lax.fori_loop(0, T, body, 0, unroll=8)
    pltpu.make_async_copy(h_scr, hg_ref.at[pl.ds(0, T), :], sem).wait()
    pltpu.make_async_copy(h_scr, hg_ref.at[pl.ds(0, T), :], sem).wait()


def _moe_dispatch(fill, pos1, pos2, x2, mod, n_rows, tm, seq):
    m, d = x2.shape
    T = min(MOE_DMA_T, seq)
    tpb = seq // T
    smem = lambda: pl.BlockSpec((T,), lambda i, fill: (i,), memory_space=pltpu.SMEM)
    grid_spec = pltpu.PrefetchScalarGridSpec(
        num_scalar_prefetch=1,
        grid=(m // T,),
        in_specs=[smem(), smem(),
                  pl.BlockSpec((T, d), lambda i, fill: (i, 0)),
                  pl.BlockSpec((None, 6, d), lambda i, fill: (i // tpb, 0, 0))],
        out_specs=pl.BlockSpec(memory_space=pl.ANY),
        scratch_shapes=[pltpu.VMEM((T, d), F32), pltpu.VMEM((tm, d), F32),
                        pltpu.SemaphoreType.DMA(()), pltpu.SemaphoreType.DMA(())])
    return pl.pallas_call(
        functools.partial(_moe_dispatch_kernel, tm=tm),
        grid_spec=grid_spec,
        out_shape=jax.ShapeDtypeStruct((n_rows, d), F32),
        compiler_params=_cparams(("arbitrary",)),
        name="moe_dispatch",
    )(fill, pos1, pos2, x2, mod)


def _moe_group_kernel(eid_ref, nv_ref, hg_ref, w1_ref, w3_ref, w2_ref, o_ref, h_scr, acc_scr):
    del eid_ref
    r = pl.program_id(0)
    f = pl.program_id(1)

    @pl.when(r < nv_ref[0])
    def _():
        @pl.when(f == 0)
        def _():
            h_scr[...] = hg_ref[...].astype(BF16)
            acc_scr[...] = jnp.zeros_like(acc_scr)

        h = h_scr[...]
        a = jnp.dot(h, w1_ref[...], preferred_element_type=F32)
        b = jnp.dot(h, w3_ref[...], preferred_element_type=F32)
        g = (a * _sigmoid(a) * b).astype(BF16)
        acc_scr[...] += jnp.dot(g, w2_ref[...], preferred_element_type=F32)

        @pl.when(f == pl.num_programs(1) - 1)
        def _():
            o_ref[...] = acc_scr[...]

    @pl.when((r >= nv_ref[0]) & (f == 0))
    def _():
        o_ref[...] = jnp.zeros_like(o_ref)


def _moe_group(eid, nvalid, hg, w13, w2, layer, tm):
    n_rows, d = hg.shape
    dff = w2.shape[2]
    tf = FFN_TF
    nf = dff // tf
    row = lambda r, f, eid, nv: (jnp.minimum(r, nv[0] - 1), 0)
    fe = lambda r, f, nv: jnp.where(r < nv[0], f, nf - 1)
    grid_spec = pltpu.PrefetchScalarGridSpec(
        num_scalar_prefetch=2,
        grid=(n_rows // tm, nf),
        in_specs=[pl.BlockSpec((tm, d), row),
                  pl.BlockSpec((None, None, d, tf), lambda r, f, eid, nv: (layer, eid[r], 0, fe(r, f, nv))),
                  pl.BlockSpec((None, None, d, tf),
                               lambda r, f, eid, nv: (layer, eid[r], 0, nf + fe(r, f, nv))),
                  pl.BlockSpec((None, None, tf, d), lambda r, f, eid, nv: (layer, eid[r], fe(r, f, nv), 0))],
        out_specs=pl.BlockSpec((tm, d), lambda r, f, eid, nv: (r, 0)),
        scratch_shapes=[pltpu.VMEM((tm, d), BF16), pltpu.VMEM((tm, d), F32)])
    return pl.pallas_call(
        _moe_group_kernel,
        grid_spec=grid_spec,
        out_shape=jax.ShapeDtypeStruct((n_rows, d), F32),
        compiler_params=_cparams(("arbitrary", "arbitrary")),
        name="moe_experts",
    )(eid, nvalid, hg, w13, w13, w2)


def _moe_combine_kernel(pos1_ref, pos2_ref, x_ref, mod_ref, info_ref, fg_ref, ys_ref, o_ref, y1, y2, sem,
                        *, final):
    T = x_ref.shape[0]

    def body(j, carry):
        _row_copy(ys_ref, pos1_ref[j], y1, j, sem).start()
        _row_copy(ys_ref, pos2_ref[j], y2, j, sem).start()
        return carry

    lax.fori_loop(0, T, body, 0, unroll=8)
    pltpu.make_async_copy(ys_ref.at[pl.ds(0, T), :], y1, sem).wait()
    pltpu.make_async_copy(ys_ref.at[pl.ds(0, T), :], y2, sem).wait()
    info = info_ref[...]
    y = info[:, 0:1] * y1[...] + info[:, 1:2] * y2[...]
    out = x_ref[...] + mod_ref[5:6, :] * y
    o_ref[...] = _final_norm(out, fg_ref) if final else out


def _moe_combine(pos1, pos2, x2, mod, info, final_g, ys, final, seq):
    m, d = x2.shape
    T = min(MOE_DMA_T, seq)
    tpb = seq // T
    smem = lambda: pl.BlockSpec((T,), lambda i: (i,), memory_space=pltpu.SMEM)
    return pl.pallas_call(
        functools.partial(_moe_combine_kernel, final=final),
        grid=(m // T,),
        in_specs=[smem(), smem(),
                  pl.BlockSpec((T, d), lambda i: (i, 0)),
                  pl.BlockSpec((None, 6, d), lambda i: (i // tpb, 0, 0)),
                  pl.BlockSpec((T, LANES), lambda i: (i, 0)),
                  pl.BlockSpec((1, d), lambda i: (0, 0)),
                  pl.BlockSpec(memory_space=pl.ANY)],
        out_specs=pl.BlockSpec((T, d), lambda i: (i, 0)),
        out_shape=jax.ShapeDtypeStruct((m, d), F32),
        scratch_shapes=[pltpu.VMEM((T, d), F32), pltpu.VMEM((T, d), F32), pltpu.SemaphoreType.DMA(())],
        compiler_params=_cparams(("arbitrary",)),
        name="moe_combine",
    )(pos1, pos2, x2, mod, info, final_g, ys)


def _moe(x2, mod, rw, rb, w13, w2, layer, final_g, final, seq):
    m, d = x2.shape
    ne = w2.shape[1]
    tm = min(FFN_TM, seq)
    n_rows = 2 * m + ne * tm
    info, info_t = _router(x2, mod, rw, rb, seq)
    pos, meta = _moe_meta(info_t, tm)
    meta = meta[:, 0].astype(jnp.int32)
    off, padded, tot = (meta[k * SUBLANES:k * SUBLANES + ne] for k in range(3))
    ends = off + padded
    nvalid = ends[ne - 1] // tm
    starts = jnp.minimum(jnp.arange(n_rows // tm, dtype=jnp.int32), nvalid - 1) * tm
    eid = jnp.sum((starts[:, None] >= ends[None, :]).astype(jnp.int32), axis=1)
    fill = jnp.concatenate([off + tot, padded - tot, jnp.stack([nvalid, n_rows // tm - nvalid])])
    hg = _moe_dispatch(fill, pos[0], pos[1], x2, mod, n_rows, tm, seq)
    ys = _moe_group(eid, nvalid.reshape(1), hg, w13, w2, layer, tm)
    return _moe_combine(pos[0], pos[1], x2, mod, info, final_g, ys, final, seq)


def kernel(x, c, ada_w, ada_b, norm_g, ml_w_in, ml_gate_b, ml_head_g, ml_w_out, ffn_w13, ffn_w2,
           lru_w_in, lru_conv_w, lru_conv_b, lru_gate_w, lru_gate_b, lru_a_param, lru_w_out,
           moe_router_w, moe_router_b, moe_w13, moe_w2, final_norm_g):
    batch, seq, d = x.shape
    depth = ada_w.shape[0]
    m = batch * seq
    qkw = ML_HEADS * ML_DQK
    vw = ML_HEADS * ML_DV

    ones = jnp.ones((depth, 1, d), F32)
    zeros = jnp.zeros((depth, 1, d), F32)
    g0, g1 = norm_g[:, 0:1, :], norm_g[:, 1:2, :]
    mul = jnp.concatenate([ones, g0, ones, ones, g1, ones], axis=1).reshape(depth, 6, 1, d)
    add = jnp.concatenate([zeros, g0, zeros, zeros, g1, zeros], axis=1).reshape(depth, 6, 1, d)
    mods = _ada(c, ada_w, ada_b, mul, add)

    ffn_w13_b, ffn_w2_b = ffn_w13.astype(BF16), ffn_w2.astype(BF16)
    moe_w13_b, moe_w2_b = moe_w13.astype(BF16), moe_w2.astype(BF16)

    x2 = x.reshape(m, d)
    for i in range(depth):
        j = i // 2
        mod = mods[i]
        last = i == depth - 1
        if i % 2 == 0:
            w_in = ml_w_in[j]
            wq = w_in[:, :qkw].astype(BF16)
            wkt = w_in[:, qkw:2 * qkw].T.astype(BF16)
            wvo = w_in[:, 2 * qkw:2 * qkw + 2 * vw].astype(BF16)
            wg = jnp.pad(w_in[:, 2 * qkw + 2 * vw:], ((0, 0), (0, LANES - 2 * ML_HEADS))).astype(BF16)
            bg = jnp.pad(ml_gate_b[j], (0, LANES - 2 * ML_HEADS)).reshape(1, LANES)
            q, kt, vo, gates = _mlproj(x2, mod, wq, wkt, wvo, wg, bg, seq)
            cols = _gateprep(gates, min(ML_CHUNK, seq), seq)
            a_rows = cols[:, :ML_HEADS].T
            x2 = _mlstm(x2, mod, q, kt, vo, a_rows, cols, ml_head_g[j].reshape(1, vw),
                        ml_w_out[j].astype(BF16), batch, seq)
            x2 = _ffn(x2, mod, ffn_w13_b, ffn_w2_b, j, seq)
        else:
            gx = _lruproj(x2, mod, lru_w_in[j].astype(BF16), seq)
            gw = lru_gate_w[j]
            gb = lru_gate_b[j]
            wgx = gw[:, :, :LRU_BW].astype(BF16)
            wga = gw[:, :, LRU_BW:].astype(BF16)
            bgx = gb[:, :LRU_BW].reshape(1, -1)
            bga = gb[:, LRU_BW:].reshape(1, -1)
            x2 = _lru(x2, mod, gx, lru_conv_w[j], lru_conv_b[j].reshape(1, -1), wgx, wga, bgx, bga,
                      lru_a_param[j].reshape(1, -1), lru_w_out[j].astype(BF16), batch, seq)
            x2 = _moe(x2, mod, moe_router_w[j], moe_router_b[j], moe_w13_b, moe_w2_b, j, final_norm_g.reshape(1, d), last, seq)
    return x2.reshape(batch, seq, d)
```

```python
import functools

import jax
import jax.numpy as jnp
from jax import lax
from jax.experimental import pallas as pl
from jax.experimental.pallas import tpu as pltpu

F32 = jnp.float32
BF16 = jnp.bfloat16
HIGHEST = lax.Precision.HIGHEST

RMS_EPS = 1e-6
ML_HEADS = 8
ML_DQK = 64
ML_DV = 128
LRU_HEADS = 4
LRU_BW = 256
CONV_W = 4
LRU_C = 8.0
N_EXPERTS = 8

LANES = 128
SUBLANES = 8
VMEM_LIMIT = 56 * 1024 * 1024

ML_CHUNK = 256
LRU_TILE = 256
PROJ_TM = 512
FFN_TM = 1024
FFN_TF = 512
ROUTER_TM = 1024
GATEPREP_ROWS = 1024
MOE_META_T = 512
MOE_DMA_T = 1024


def _cparams(sem):
    return pltpu.CompilerParams(dimension_semantics=sem, vmem_limit_bytes=VMEM_LIMIT)


def _norm_mod(x, mod_ref, row):
    r = lax.rsqrt(jnp.mean(x * x, axis=-1, keepdims=True) + RMS_EPS)
    return (x * r) * mod_ref[row + 1:row + 2, :] + mod_ref[row:row + 1, :]


def _sigmoid(x):
    return 1.0 / (1.0 + jnp.exp(-x))


def _log_sigmoid(x):
    return jnp.minimum(x, 0.0) - jnp.log1p(jnp.exp(-jnp.abs(x)))


def _ada_kernel(c_ref, w_ref, b_ref, mul_ref, add_ref, o_ref):
    c = c_ref[...]
    ca = (c * _sigmoid(c)).astype(BF16)
    mod = jnp.dot(ca, w_ref[...].astype(BF16), preferred_element_type=F32) + b_ref[...]
    o_ref[...] = mul_ref[...] * mod + add_ref[...]


def _ada(c, ada_w, ada_b, mul, add):
    depth, d, _ = ada_w.shape
    b = c.shape[0]
    row = pl.BlockSpec((None, None, 1, d), lambda i, j: (i, j, 0, 0))
    out = pl.pallas_call(
        _ada_kernel,
        grid=(depth, 6),
        in_specs=[pl.BlockSpec((b, d), lambda i, j: (0, 0)),
                  pl.BlockSpec((None, d, d), lambda i, j: (i, 0, j)),
                  row, row, row],
        out_specs=pl.BlockSpec((None, None, b, d), lambda i, j: (i, j, 0, 0)),
        out_shape=jax.ShapeDtypeStruct((depth, 6, b, d), F32),
        compiler_params=_cparams(("arbitrary", "arbitrary")),
        name="ada_mod",
    )(c, ada_w, ada_b.reshape(depth, 6, 1, d), mul, add)
    return out.transpose(0, 2, 1, 3)


def _mlproj_kernel(x_ref, mod_ref, wq_ref, wkt_ref, wvo_ref, wg_ref, bg_ref,
                   q_ref, kt_ref, vo_ref, g_ref):
    h = _norm_mod(x_ref[...], mod_ref, 0).astype(BF16)
    q = jnp.dot(h, wq_ref[...], preferred_element_type=F32)
    q_ref[...] = (q * (ML_DQK ** -0.5)).astype(BF16)
    kt = lax.dot_general(wkt_ref[...], h, (((1,), (1,)), ((), ())), preferred_element_type=F32)
    kt_ref[...] = kt.astype(BF16)
    vo_ref[...] = jnp.dot(h, wvo_ref[...], preferred_element_type=F32).astype(BF16)
    g_ref[...] = jnp.dot(h, wg_ref[...], preferred_element_type=F32) + bg_ref[...]


def _mlproj(x2, mod, wq, wkt, wvo, wg, bg, seq):
    m, d = x2.shape
    tm = min(PROJ_TM, seq)
    tpb = seq // tm
    nq, nvo = wq.shape[1], wvo.shape[1]
    full = lambda shape: pl.BlockSpec(shape, lambda i: (0,) * len(shape))
    return pl.pallas_call(
        _mlproj_kernel,
        grid=(m // tm,),
        in_specs=[pl.BlockSpec((tm, d), lambda i: (i, 0)),
                  pl.BlockSpec((None, 6, d), lambda i: (i // tpb, 0, 0)),
                  full(wq.shape), full(wkt.shape), full(wvo.shape), full(wg.shape), full(bg.shape)],
        out_specs=[pl.BlockSpec((tm, nq), lambda i: (i, 0)),
                   pl.BlockSpec((nq, tm), lambda i: (0, i)),
                   pl.BlockSpec((tm, nvo), lambda i: (i, 0)),
                   pl.BlockSpec((tm, LANES), lambda i: (i, 0))],
        out_shape=[jax.ShapeDtypeStruct((m, nq), BF16),
                   jax.ShapeDtypeStruct((nq, m), BF16),
                   jax.ShapeDtypeStruct((m, nvo), BF16),
                   jax.ShapeDtypeStruct((m, LANES), F32)],
        compiler_params=_cparams(("parallel",)),
        name="mlstm_proj",
    )(x2, mod, wq, wkt, wvo, wg, bg)


def _gateprep_kernel(g_ref, o_ref, *, chunk):
    n = chunk
    r = lax.broadcasted_iota(jnp.int32, (n, n), 0)
    s = lax.broadcasted_iota(jnp.int32, (n, n), 1)
    tril = jnp.where(s <= r, 1.0, 0.0).astype(F32)
    lane = lax.broadcasted_iota(jnp.int32, (n, LANES), 1)
    for c in range(g_ref.shape[0] // chunk):
        g = g_ref[c * chunk:(c + 1) * chunk, :]
        b = jnp.dot(tril, _log_sigmoid(g), precision=HIGHEST, preferred_element_type=F32)
        b_i = pltpu.roll(b, LANES - ML_HEADS, axis=1)
        o_ref[c * chunk:(c + 1) * chunk, :] = jnp.where(lane < ML_HEADS, g - b_i, b)


def _gateprep(gates, chunk, seq):
    m = gates.shape[0]
    rows = min(GATEPREP_ROWS, seq)
    return pl.pallas_call(
        functools.partial(_gateprep_kernel, chunk=chunk),
        grid=(m // rows,),
        in_specs=[pl.BlockSpec((rows, LANES), lambda i: (i, 0))],
        out_specs=pl.BlockSpec((rows, LANES), lambda i: (i, 0)),
        out_shape=jax.ShapeDtypeStruct((m, LANES), F32),
        compiler_params=_cparams(("parallel",)),
        name="mlstm_gateprep",
    )(gates)


def _mlstm_kernel(x_ref, mod_ref, q_ref, kt_ref, vo_ref, arow_ref, col_ref, hg_ref, wout_ref,
                  o_ref, c_scr, m_scr, hs_scr):
    L = q_ref.shape[0]
    vw = ML_HEADS * ML_DV

    @pl.when(pl.program_id(1) == 0)
    def _():
        c_scr[...] = jnp.zeros_like(c_scr)
        m_scr[...] = jnp.zeros_like(m_scr)

    t_idx = lax.broadcasted_iota(jnp.int32, (L, L), 0)
    s_idx = lax.broadcasted_iota(jnp.int32, (L, L), 1)
    causal = s_idx <= t_idx
    lane_q = lax.broadcasted_iota(jnp.int32, (1, 2 * ML_DQK), 1)
    head_mask = [jnp.where(lane_q < ML_DQK, 1.0, 0.0).astype(BF16),
                 jnp.where(lane_q >= ML_DQK, 1.0, 0.0).astype(BF16)]
    lane_v = lax.broadcasted_iota(jnp.int32, (L, ML_DV), 1)
    ones_blk = jnp.ones((L, ML_DV), BF16)

    heads = range(ML_HEADS)
    m_all = [m_scr[h] for h in heads]
    c_all = [c_scr[h] for h in heads]
    a_rows = [arow_ref[h:h + 1, :] for h in heads]
    b_cols = [col_ref[:, ML_HEADS + h:ML_HEADS + h + 1] for h in heads]

    a_mats = [jnp.where(causal, a_rows[h], -jnp.inf) for h in heads]
    m_cols = [jnp.maximum(m_all[h], jnp.max(a_mats[h], axis=1, keepdims=True)) for h in heads]
    ws = [jnp.exp(a_mats[h] - m_cols[h]) for h in heads]
    e_cols = [jnp.exp(m_all[h] - m_cols[h]) for h in heads]
    floors = [jnp.exp(-(b_cols[h] + m_cols[h])) for h in heads]

    v_augs, ss, inters = [], [], []
    for h in heads:
        p, half = h // 2, h % 2
        qm = q_ref[:, p * 128:(p + 1) * 128] * head_mask[half]
        kt2 = kt_ref[p * 128:(p + 1) * 128, :]
        v_augs.append(jnp.concatenate([vo_ref[:, h * ML_DV:(h + 1) * ML_DV], ones_blk], axis=1))
        ss.append((jnp.dot(qm, kt2, preferred_element_type=F32) * ws[h]).astype(BF16))
        c_pair = jnp.concatenate([c_all[h], c_all[h]], axis=0).astype(BF16)
        inters.append(jnp.dot(qm, c_pair, preferred_element_type=F32))

    hhs = []
    for h in heads:
        num_aug = jnp.dot(ss[h], v_augs[h], preferred_element_type=F32) + e_cols[h] * inters[h]
        hhs.append(num_aug[:, :ML_DV] / jnp.maximum(jnp.abs(num_aug[:, ML_DV:2 * ML_DV]), floors[h]))

    rs = [lax.rsqrt(jnp.mean(hhs[h] * hhs[h], axis=-1, keepdims=True) + RMS_EPS) for h in heads]
    hs_all = []
    for h in heads:
        og = vo_ref[:, vw + h * ML_DV: vw + (h + 1) * ML_DV]
        hn = hhs[h] * rs[h] * hg_ref[:, h * ML_DV:(h + 1) * ML_DV] * _sigmoid(og.astype(F32))
        hs_all.append(hn.astype(BF16))

    c_new, m_new = [], []
    for h in heads:
        m_last = m_cols[h][L - 1:L, :]
        w_s = jnp.exp(a_rows[h] - m_last)
        decay = jnp.exp(m_all[h] - m_last)
        kw = (kt_ref[h * ML_DQK:(h + 1) * ML_DQK, :].astype(F32) * w_s).astype(BF16)
        c_new.append(decay * c_all[h] + jnp.dot(kw, v_augs[h], preferred_element_type=F32))
        m_new.append(b_cols[h][L - 1:L, :] + m_last)

    for h in range(ML_HEADS):
        c_scr[h] = c_new[h]
        m_scr[h] = m_new[h]
    mix = jnp.dot(jnp.concatenate(hs_all, axis=1), wout_ref[...], preferred_element_type=F32)
    o_ref[...] = x_ref[...] + mod_ref[2:3, :] * mix


def _mlstm(x2, mod, q, kt, vo, a_rows, cols, head_g, w_out, batch, seq):
    m, d = x2.shape
    L = min(ML_CHUNK, seq)
    nc = seq // L
    nq, nvo = q.shape[1], vo.shape[1]
    row_blk = lambda w: pl.BlockSpec((L, w), lambda b, c: (b * nc + c, 0))
    return pl.pallas_call(
        _mlstm_kernel,
        grid=(batch, nc),
        in_specs=[row_blk(d),
                  pl.BlockSpec((None, 6, d), lambda b, c: (b, 0, 0)),
                  row_blk(nq),
                  pl.BlockSpec((nq, L), lambda b, c: (0, b * nc + c)),
                  row_blk(nvo),
                  pl.BlockSpec((ML_HEADS, L), lambda b, c: (0, b * nc + c)),
                  row_blk(LANES),
                  pl.BlockSpec((1, d), lambda b, c: (0, 0)),
                  pl.BlockSpec((d, d), lambda b, c: (0, 0))],
        out_specs=row_blk(d),
        out_shape=jax.ShapeDtypeStruct((m, d), F32),
        scratch_shapes=[pltpu.VMEM((ML_HEADS, ML_DQK, 2 * ML_DV), F32),
                        pltpu.VMEM((ML_HEADS, 1, 1), F32),
                        pltpu.VMEM((L, d), BF16)],
        compiler_params=_cparams(("parallel", "arbitrary")),
        name="mlstm_core",
    )(x2, mod, q, kt, vo, a_rows, cols, head_g, w_out)


def _lru_kernel(x_ref, mod_ref, gx_ref, cw_ref, cb_ref, wgx_ref, wga_ref, bgx_ref, bga_ref, ap_ref,
                wout_ref, o_ref, xbuf, a_scr, u_scr, h_scr, hc_scr):
    T, W = h_scr.shape
    PAD = SUBLANES

    @pl.when(pl.program_id(1) == 0)
    def _():
        xbuf[0:PAD, :] = jnp.zeros((PAD, W), F32)
        hc_scr[...] = jnp.zeros_like(hc_scr)

    xbuf[PAD:PAD + T, :] = gx_ref[:, W:2 * W].astype(F32)
    xc = cb_ref[...] + xbuf[PAD - 3:PAD - 3 + T, :] * cw_ref[0:1, :]
    for j in range(1, CONV_W):
        xc = xc + xbuf[PAD - 3 + j:PAD - 3 + j + T, :] * cw_ref[j:j + 1, :]
    xbuf[0:PAD, :] = xbuf[T:T + PAD, :]

    xcb = xc.astype(BF16)
    sp = ap_ref[...]
    sp = jnp.maximum(-sp, 0.0) + jnp.log1p(jnp.exp(-jnp.abs(sp)))
    for hd in range(LRU_HEADS):
        sl = slice(hd * LRU_BW, (hd + 1) * LRU_BW)
        xh = xcb[:, sl]
        gxh = _sigmoid(jnp.dot(xh, wgx_ref[hd], preferred_element_type=F32) + bgx_ref[:, sl])
        gah = _sigmoid(jnp.dot(xh, wga_ref[hd], preferred_element_type=F32) + bga_ref[:, sl])
        log_a = (-LRU_C) * gah * sp[:, sl]
        a = jnp.exp(log_a)
        a_scr[:, sl] = a
        u_scr[:, sl] = xc[:, sl] * gxh * jnp.sqrt(jnp.tanh(-log_a) * (1.0 + a * a))

    row = lax.broadcasted_iota(jnp.int32, (SUBLANES, W), 0)

    first = row == 0

    def body(r, hc):
        base = pl.multiple_of(r * SUBLANES, SUBLANES)
        a8 = a_scr[pl.ds(base, SUBLANES), :]
        u8 = u_scr[pl.ds(base, SUBLANES), :]
        u8 = u8 + jnp.where(first, a8 * hc, 0.0)
        a8 = jnp.where(first, 0.0, a8)
        for sft in (1, 2, 4):
            u8 = a8 * pltpu.roll(u8, sft, axis=0) + u8
            if sft < 4:
                a8 = a8 * pltpu.roll(a8, sft, axis=0)
        h_scr[pl.ds(base, SUBLANES), :] = u8
        return u8[SUBLANES - 1:SUBLANES, :]

    hc_scr[...] = lax.fori_loop(0, T // SUBLANES, body, hc_scr[...], unroll=2)

    gb = gx_ref[:, 0:W].astype(F32)
    y = (h_scr[...] * jax.nn.gelu(gb)).astype(BF16)
    mix = jnp.dot(y, wout_ref[...], preferred_element_type=F32)
    o_ref[...] = x_ref[...] + mod_ref[2:3, :] * mix


def _lru(x2, mod, gx, conv_w, conv_b, wgx, wga, bgx, bga, a_param, w_out, batch, seq):
    m, d = x2.shape
    w = w_out.shape[0]
    T = min(LRU_TILE, seq)
    nt = seq // T
    row_blk = lambda wd: pl.BlockSpec((T, wd), lambda b, t: (b * nt + t, 0))
    full = lambda a: pl.BlockSpec(a.shape, lambda b, t: (0,) * a.ndim)
    return pl.pallas_call(
        _lru_kernel,
        grid=(batch, nt),
        in_specs=[row_blk(d),
                  pl.BlockSpec((None, 6, d), lambda b, t: (b, 0, 0)),
                  row_blk(2 * w),
                  full(conv_w), full(conv_b), full(wgx), full(wga), full(bgx), full(bga), full(a_param),
                  full(w_out)],
        out_specs=row_blk(d),
        out_shape=jax.ShapeDtypeStruct((m, d), F32),
        scratch_shapes=[pltpu.VMEM((T + SUBLANES, w), F32),
                        pltpu.VMEM((T, w), F32),
                        pltpu.VMEM((T, w), F32),
                        pltpu.VMEM((T, w), F32),
                        pltpu.VMEM((1, w), F32)],
        compiler_params=_cparams(("parallel", "arbitrary")),
        name="rglru_core",
    )(x2, mod, gx, conv_w, conv_b, wgx, wga, bgx, bga, a_param, w_out)


def _lruproj_kernel(x_ref, mod_ref, w_ref, o_ref):
    h = _norm_mod(x_ref[...], mod_ref, 0).astype(BF16)
    o_ref[...] = jnp.dot(h, w_ref[...], preferred_element_type=F32).astype(BF16)


def _lruproj(x2, mod, w, seq):
    m, d = x2.shape
    n = w.shape[1]
    tm = min(PROJ_TM, seq)
    tpb = seq // tm
    return pl.pallas_call(
        _lruproj_kernel,
        grid=(m // tm,),
        in_specs=[pl.BlockSpec((tm, d), lambda i: (i, 0)),
                  pl.BlockSpec((None, 6, d), lambda i: (i // tpb, 0, 0)),
                  pl.BlockSpec((d, n), lambda i: (0, 0))],
        out_specs=pl.BlockSpec((tm, n), lambda i: (i, 0)),
        out_shape=jax.ShapeDtypeStruct((m, n), BF16),
        compiler_params=_cparams(("parallel",)),
        name="rglru_proj",
    )(x2, mod, w)


def _final_norm(x, g_ref):
    return x * lax.rsqrt(jnp.mean(x * x, axis=-1, keepdims=True) + RMS_EPS) * g_ref[...]


def _swiglu(h, w13_ref, w2_ref):
    dff = w2_ref.shape[0]
    acc = None
    for c in range(dff // FFN_TF):
        lo, hi = c * FFN_TF, (c + 1) * FFN_TF
        a = jnp.dot(h, w13_ref[:, lo:hi], preferred_element_type=F32)
        b = jnp.dot(h, w13_ref[:, dff + lo:dff + hi], preferred_element_type=F32)
        g = (a * _sigmoid(a) * b).astype(BF16)
        y = jnp.dot(g, w2_ref[lo:hi, :], preferred_element_type=F32)
        acc = y if acc is None else acc + y
    return acc


def _ffn_kernel(x_ref, mod_ref, w13_ref, w2_ref, o_ref):
    x = x_ref[...]
    h = _norm_mod(x, mod_ref, 3).astype(BF16)
    o_ref[...] = x + mod_ref[5:6, :] * _swiglu(h, w13_ref, w2_ref)


def _ffn(x2, mod, w13, w2, layer, seq):
    m, d = x2.shape
    dff = w2.shape[1]
    tm = min(FFN_TM, seq)
    tpb = seq // tm
    return pl.pallas_call(
        _ffn_kernel,
        grid=(m // tm,),
        in_specs=[pl.BlockSpec((tm, d), lambda i: (i, 0)),
                  pl.BlockSpec((None, 6, d), lambda i: (i // tpb, 0, 0)),
                  pl.BlockSpec((None, d, 2 * dff), lambda i: (layer, 0, 0), pipeline_mode=pl.Buffered(1)),
                  pl.BlockSpec((None, dff, d), lambda i: (layer, 0, 0), pipeline_mode=pl.Buffered(1))],
        out_specs=pl.BlockSpec((tm, d), lambda i: (i, 0)),
        out_shape=jax.ShapeDtypeStruct((m, d), F32),
        compiler_params=_cparams(("parallel",)),
        name="ffn_swiglu",
    )(x2, mod, w13, w2)


def _router_kernel(x_ref, mod_ref, rw_ref, rwh_ref, rb_ref, info_ref, info_t_ref):
    h = _norm_mod(x_ref[...], mod_ref, 3)
    h_hi = h.astype(BF16)
    h_lo = (h - h_hi.astype(F32)).astype(BF16)
    p_hi = jnp.dot(h_hi, rw_ref[...], preferred_element_type=F32)
    p_lo = jnp.dot(h_lo, rwh_ref[...], preferred_element_type=F32)
    logits = p_hi + pltpu.roll(p_hi, LANES - N_EXPERTS, axis=1) + p_lo + rb_ref[...]
    lt = logits.T[0:N_EXPERTS, :]
    tm = lt.shape[1]
    sub = lax.broadcasted_iota(jnp.int32, lt.shape, 0).astype(F32)
    v1 = jnp.max(lt, axis=0, keepdims=True)
    i1 = jnp.min(jnp.where(lt == v1, sub, float(N_EXPERTS)), axis=0, keepdims=True)
    rest = jnp.where(sub == i1, -jnp.inf, lt)
    v2 = jnp.max(rest, axis=0, keepdims=True)
    i2 = jnp.min(jnp.where(rest == v2, sub, float(N_EXPERTS)), axis=0, keepdims=True)
    e2 = jnp.exp(v2 - v1)
    w1 = 1.0 / (1.0 + e2)
    w2 = e2 / (1.0 + e2)
    info_t = jnp.where(sub == 0.0, w1, jnp.where(sub == 1.0, w2, jnp.where(sub == 2.0, i1,
                                                                          jnp.where(sub == 3.0, i2, 0.0))))
    info_t_ref[...] = info_t
    info_ref[...] = jnp.concatenate([info_t, jnp.zeros((LANES - N_EXPERTS, tm), F32)], axis=0).T


def _router(x2, mod, router_w, router_b, seq):
    m, d = x2.shape
    tm = min(ROUTER_TM, seq)
    tpb = seq // tm
    ne = router_w.shape[1]
    w_hi = router_w.astype(BF16)
    w_lo = (router_w - w_hi.astype(F32)).astype(BF16)
    rw = jnp.pad(jnp.concatenate([w_hi, w_lo], axis=1), ((0, 0), (0, LANES - 2 * ne)))
    rwh = jnp.pad(w_hi, ((0, 0), (0, LANES - ne)))
    rb = jnp.pad(router_b, (0, LANES - ne), constant_values=-jnp.inf).reshape(1, LANES)
    return pl.pallas_call(
        _router_kernel,
        grid=(m // tm,),
        in_specs=[pl.BlockSpec((tm, d), lambda i: (i, 0)),
                  pl.BlockSpec((None, 6, d), lambda i: (i // tpb, 0, 0)),
                  pl.BlockSpec((d, LANES), lambda i: (0, 0)),
                  pl.BlockSpec((d, LANES), lambda i: (0, 0)),
                  pl.BlockSpec((1, LANES), lambda i: (0, 0))],
        out_specs=[pl.BlockSpec((tm, LANES), lambda i: (i, 0)),
                   pl.BlockSpec((SUBLANES, tm), lambda i: (0, i))],
        out_shape=[jax.ShapeDtypeStruct((m, LANES), F32),
                   jax.ShapeDtypeStruct((SUBLANES, m), F32)],
        compiler_params=_cparams(("parallel",)),
        name="moe_router",
    )(x2, mod, rw, rwh, rb)


def _moe_meta_kernel(sel_ref, tri_ref, pos_ref, meta_ref, run1, run2, off1, off2, *, tm):
    p = pl.program_id(0)
    i = pl.program_id(1)
    T = sel_ref.shape[1]
    sub = lax.broadcasted_iota(jnp.int32, (SUBLANES, T), 0).astype(F32)
    oh1 = jnp.where(sel_ref[2:3, :] == sub, 1.0, 0.0)
    oh2 = jnp.where(sel_ref[3:4, :] == sub, 1.0, 0.0)
    c1 = jnp.sum(oh1, axis=1, keepdims=True)
    c2 = jnp.sum(oh2, axis=1, keepdims=True)

    @pl.when((p == 0) & (i == 0))
    def _():
        run1[...] = jnp.zeros_like(run1)
        run2[...] = jnp.zeros_like(run2)

    @pl.when((p == 1) & (i == 0))
    def _():
        cnt1 = run1[...]
        tot = cnt1 + run2[...]
        padded = jnp.floor((tot + (tm - 1)) * (1.0 / tm)) * tm
        e_idx = lax.broadcasted_iota(jnp.int32, padded.shape, 0)
        off = jnp.zeros_like(padded)
        for e in range(N_EXPERTS - 1):
            off = off + jnp.where(e_idx > e, padded[e:e + 1, :], 0.0)
        off1[...] = off
        off2[...] = off + cnt1
        meta_ref[0:SUBLANES, :] = off
        meta_ref[SUBLANES:2 * SUBLANES, :] = padded
        meta_ref[2 * SUBLANES:3 * SUBLANES, :] = tot
        run1[...] = jnp.zeros_like(run1)
        run2[...] = jnp.zeros_like(run2)

    @pl.when(p == 1)
    def _():
        tri = tri_ref[...]
        pre1 = jnp.dot(oh1.astype(BF16), tri, preferred_element_type=F32)
        pre2 = jnp.dot(oh2.astype(BF16), tri, preferred_element_type=F32)
        pos1 = jnp.sum(oh1 * (pre1 + (off1[:, 0:1] + run1[:, 0:1])), axis=0, keepdims=True)
        pos2 = jnp.sum(oh2 * (pre2 + (off2[:, 0:1] + run2[:, 0:1])), axis=0, keepdims=True)
        row = lax.broadcasted_iota(jnp.int32, (SUBLANES, T), 0)
        pos = jnp.where(row == 0, pos1, jnp.where(row == 1, pos2, 0.0))
        pos_ref[...] = pos.astype(jnp.int32)

    run1[...] += c1
    run2[...] += c2


def _moe_meta(info_t, tm):
    m = info_t.shape[1]
    T = min(MOE_META_T, m)
    s_i = lax.broadcasted_iota(jnp.int32, (T, T), 0)
    t_i = lax.broadcasted_iota(jnp.int32, (T, T), 1)
    tri = (s_i < t_i).astype(BF16)
    return pl.pallas_call(
        functools.partial(_moe_meta_kernel, tm=tm),
        grid=(2, m // T),
        in_specs=[pl.BlockSpec((SUBLANES, T), lambda p, i: (0, i)),
                  pl.BlockSpec((T, T), lambda p, i: (0, 0))],
        out_specs=[pl.BlockSpec((SUBLANES, T), lambda p, i: (0, i * p)),
                   pl.BlockSpec((3 * SUBLANES, LANES), lambda p, i: (0, 0))],
        out_shape=[jax.ShapeDtypeStruct((SUBLANES, m), jnp.int32),
                   jax.ShapeDtypeStruct((3 * SUBLANES, LANES), F32)],
        scratch_shapes=[pltpu.VMEM((SUBLANES, LANES), F32)] * 4,
        compiler_params=_cparams(("arbitrary", "arbitrary")),
        name="moe_meta",
    )(info_t, tri)


def _row_copy(src_ref, src_row, dst_ref, dst_row, sem):
    return pltpu.make_async_copy(src_ref.at[pl.ds(src_row, 1), :], dst_ref.at[pl.ds(dst_row, 1), :], sem)


def _moe_dispatch_kernel(fill_ref, pos1_ref, pos2_ref, x_ref, mod_ref, hg_ref, h_scr, z_scr, sem, zsem, *, tm):
    T = x_ref.shape[0]
    ne = N_EXPERTS

    def zero_row(e, r):
        return _row_copy(z_scr, 0, hg_ref, fill_ref[e] + r, zsem)

    def zero_tile(t):
        row0 = pl.multiple_of((fill_ref[2 * ne] + t) * tm, tm)
        return pltpu.make_async_copy(z_scr, hg_ref.at[pl.ds(row0, tm), :], zsem)

    def for_each_fill(fn):
        for e in range(ne):
            lax.fori_loop(0, fill_ref[ne + e], lambda r, c, e=e: (fn(zero_row(e, r)), c)[1], 0)
        lax.fori_loop(0, fill_ref[2 * ne + 1], lambda t, c: (fn(zero_tile(t)), c)[1], 0)

    @pl.when(pl.program_id(0) == 0)
    def _():
        z_scr[...] = jnp.zeros_like(z_scr)
        for_each_fill(lambda cp: cp.start())
        for_each_fill(lambda cp: cp.wait())

    h_scr[...] = _norm_mod(x_ref[...], mod_ref, 3)

    def body(j, carry):
        _row_copy(h_scr, j, hg_ref, pos1_ref[j], sem).start()
        _row_copy(h_scr, j, hg_ref, pos2_ref[j], sem).start()
        return carry

    lax.fori_loop(0, T, body, 0, unroll=8)
    pltpu.make_async_copy(h_scr, hg_ref.at[pl.ds(0, T), :], sem).wait()
    pltpu.make_async_copy(h_scr, hg_ref.at[pl.ds(0, T), :], sem).wait()


def _moe_dispatch(fill, pos1, pos2, x2, mod, n_rows, tm, seq):
    m, d = x2.shape
    T = min(MOE_DMA_T, seq)
    tpb = seq // T
    smem = lambda: pl.BlockSpec((T,), lambda i, fill: (i,), memory_space=pltpu.SMEM)
    grid_spec = pltpu.PrefetchScalarGridSpec(
        num_scalar_prefetch=1,
        grid=(m // T,),
        in_specs=[smem(), smem(),
                  pl.BlockSpec((T, d), lambda i, fill: (i, 0)),
                  pl.BlockSpec((None, 6, d), lambda i, fill: (i // tpb, 0, 0))],
        out_specs=pl.BlockSpec(memory_space=pl.ANY),
        scratch_shapes=[pltpu.VMEM((T, d), F32), pltpu.VMEM((tm, d), F32),
                        pltpu.SemaphoreType.DMA(()), pltpu.SemaphoreType.DMA(())])
    return pl.pallas_call(
        functools.partial(_moe_dispatch_kernel, tm=tm),
        grid_spec=grid_spec,
        out_shape=jax.ShapeDtypeStruct((n_rows, d), F32),
        compiler_params=_cparams(("arbitrary",)),
        name="moe_dispatch",
    )(fill, pos1, pos2, x2, mod)


def _moe_group_kernel(eid_ref, nv_ref, hg_ref, w13_ref, w2_ref, o_ref):
    del eid_ref
    r = pl.program_id(0)

    @pl.when(r < nv_ref[0])
    def _():
        o_ref[...] = _swiglu(hg_ref[...].astype(BF16), w13_ref, w2_ref)

    @pl.when(r >= nv_ref[0])
    def _():
        o_ref[...] = jnp.zeros_like(o_ref)


def _moe_group(eid, nvalid, hg, w13, w2, layer, tm):
    n_rows, d = hg.shape
    dff = w2.shape[2]
    grid_spec = pltpu.PrefetchScalarGridSpec(
        num_scalar_prefetch=2,
        grid=(n_rows // tm,),
        in_specs=[pl.BlockSpec((tm, d), lambda r, eid, nv: (jnp.minimum(r, nv[0] - 1), 0)),
                  pl.BlockSpec((None, None, d, 2 * dff), lambda r, eid, nv: (layer, eid[r], 0, 0),
                               pipeline_mode=pl.Buffered(1)),
                  pl.BlockSpec((None, None, dff, d), lambda r, eid, nv: (layer, eid[r], 0, 0),
                               pipeline_mode=pl.Buffered(1))],
        out_specs=pl.BlockSpec((tm, d), lambda r, eid, nv: (r, 0)))
    return pl.pallas_call(
        _moe_group_kernel,
        grid_spec=grid_spec,
        out_shape=jax.ShapeDtypeStruct((n_rows, d), F32),
        compiler_params=_cparams(("arbitrary",)),
        name="moe_experts",
    )(eid, nvalid, hg, w13, w2)


def _moe_combine_kernel(pos1_ref, pos2_ref, x_ref, mod_ref, info_ref, fg_ref, ys_ref, o_ref, y1, y2, sem,
                        *, final):
    T = x_ref.shape[0]

    def body(j, carry):
        _row_copy(ys_ref, pos1_ref[j], y1, j, sem).start()
        _row_copy(ys_ref, pos2_ref[j], y2, j, sem).start()
        return carry

    lax.fori_loop(0, T, body, 0, unroll=8)
    pltpu.make_async_copy(ys_ref.at[pl.ds(0, T), :], y1, sem).wait()
    pltpu.make_async_copy(ys_ref.at[pl.ds(0, T), :], y2, sem).wait()
    info = info_ref[...]
    y = info[:, 0:1] * y1[...] + info[:, 1:2] * y2[...]
    out = x_ref[...] + mod_ref[5:6, :] * y
    o_ref[...] = _final_norm(out, fg_ref) if final else out


def _moe_combine(pos1, pos2, x2, mod, info, final_g, ys, final, seq):
    m, d = x2.shape
    T = min(MOE_DMA_T, seq)
    tpb = seq // T
    smem = lambda: pl.BlockSpec((T,), lambda i: (i,), memory_space=pltpu.SMEM)
    return pl.pallas_call(
        functools.partial(_moe_combine_kernel, final=final),
        grid=(m // T,),
        in_specs=[smem(), smem(),
                  pl.BlockSpec((T, d), lambda i: (i, 0)),
                  pl.BlockSpec((None, 6, d), lambda i: (i // tpb, 0, 0)),
                  pl.BlockSpec((T, LANES), lambda i: (i, 0)),
                  pl.BlockSpec((1, d), lambda i: (0, 0)),
                  pl.BlockSpec(memory_space=pl.ANY)],
        out_specs=pl.BlockSpec((T, d), lambda i: (i, 0)),
        out_shape=jax.ShapeDtypeStruct((m, d), F32),
        scratch_shapes=[pltpu.VMEM((T, d), F32), pltpu.VMEM((T, d), F32), pltpu.SemaphoreType.DMA(())],
        compiler_params=_cparams(("arbitrary",)),
        name="moe_combine",
    )(pos1, pos2, x2, mod, info, final_g, ys)


def _moe(x2, mod, rw, rb, w13, w2, layer, final_g, final, seq):
    m, d = x2.shape
    ne = w2.shape[1]
    tm = min(FFN_TM, seq)
    n_rows = 2 * m + ne * tm
    info, info_t = _router(x2, mod, rw, rb, seq)
    pos, meta = _moe_meta(info_t, tm)
    meta = meta[:, 0].astype(jnp.int32)
    off, padded, tot = (meta[k * SUBLANES:k * SUBLANES + ne] for k in range(3))
    ends = off + padded
    nvalid = ends[ne - 1] // tm
    starts = jnp.minimum(jnp.arange(n_rows // tm, dtype=jnp.int32), nvalid - 1) * tm
    eid = jnp.sum((starts[:, None] >= ends[None, :]).astype(jnp.int32), axis=1)
    fill = jnp.concatenate([off + tot, padded - tot, jnp.stack([nvalid, n_rows // tm - nvalid])])
    hg = _moe_dispatch(fill, pos[0], pos[1], x2, mod, n_rows, tm, seq)
    ys = _moe_group(eid, nvalid.reshape(1), hg, w13, w2, layer, tm)
    return _moe_combine(pos[0], pos[1], x2, mod, info, final_g, ys, final, seq)


def kernel(x, c, ada_w, ada_b, norm_g, ml_w_in, ml_gate_b, ml_head_g, ml_w_out, ffn_w13, ffn_w2,
           lru_w_in, lru_conv_w, lru_conv_b, lru_gate_w, lru_gate_b, lru_a_param, lru_w_out,
           moe_router_w, moe_router_b, moe_w13, moe_w2, final_norm_g):
    batch, seq, d = x.shape
    depth = ada_w.shape[0]
    m = batch * seq
    qkw = ML_HEADS * ML_DQK
    vw = ML_HEADS * ML_DV

    ones = jnp.ones((depth, 1, d), F32)
    zeros = jnp.zeros((depth, 1, d), F32)
    g0, g1 = norm_g[:, 0:1, :], norm_g[:, 1:2, :]
    mul = jnp.concatenate([ones, g0, ones, ones, g1, ones], axis=1).reshape(depth, 6, 1, d)
    add = jnp.concatenate([zeros, g0, zeros, zeros, g1, zeros], axis=1).reshape(depth, 6, 1, d)
    mods = _ada(c, ada_w, ada_b, mul, add)

    ffn_w13_b, ffn_w2_b = ffn_w13.astype(BF16), ffn_w2.astype(BF16)
    moe_w13_b, moe_w2_b = moe_w13.astype(BF16), moe_w2.astype(BF16)

    x2 = x.reshape(m, d)
    for i in range(depth):
        j = i // 2
        mod = mods[i]
        last = i == depth - 1
        if i % 2 == 0:
            w_in = ml_w_in[j]
            wq = w_in[:, :qkw].astype(BF16)
            wkt = w_in[:, qkw:2 * qkw].T.astype(BF16)
            wvo = w_in[:, 2 * qkw:2 * qkw + 2 * vw].astype(BF16)
            wg = jnp.pad(w_in[:, 2 * qkw + 2 * vw:], ((0, 0), (0, LANES - 2 * ML_HEADS))).astype(BF16)
            bg = jnp.pad(ml_gate_b[j], (0, LANES - 2 * ML_HEADS)).reshape(1, LANES)
            q, kt, vo, gates = _mlproj(x2, mod, wq, wkt, wvo, wg, bg, seq)
            cols = _gateprep(gates, min(ML_CHUNK, seq), seq)
            a_rows = cols[:, :ML_HEADS].T
            x2 = _mlstm(x2, mod, q, kt, vo, a_rows, cols, ml_head_g[j].reshape(1, vw),
                        ml_w_out[j].astype(BF16), batch, seq)
            x2 = _ffn(x2, mod, ffn_w13_b, ffn_w2_b, j, seq)
        else:
            gx = _lruproj(x2, mod, lru_w_in[j].astype(BF16), seq)
            gw = lru_gate_w[j]
            gb = lru_gate_b[j]
            wgx = gw[:, :, :LRU_BW].astype(BF16)
            wga = gw[:, :, LRU_BW:].astype(BF16)
            bgx = gb[:, :LRU_BW].reshape(1, -1)
            bga = gb[:, LRU_BW:].reshape(1, -1)
            x2 = _lru(x2, mod, gx, lru_conv_w[j], lru_conv_b[j].reshape(1, -1), wgx, wga, bgx, bga,
                      lru_a_param[j].reshape(1, -1), lru_w_out[j].astype(BF16), batch, seq)
            x2 = _moe(x2, mod, moe_router_w[j], moe_router_b[j], moe_w13_b, moe_w2_b, j, final_norm_g.reshape(1, d), last, seq)
    return x2.reshape(batch, seq, d)
```

```python
import functools

import jax
import jax.numpy as jnp
from jax import lax
from jax.experimental import pallas as pl
from jax.experimental.pallas import tpu as pltpu

F32 = jnp.float32
BF16 = jnp.bfloat16
HIGHEST = lax.Precision.HIGHEST

RMS_EPS = 1e-6
ML_HEADS = 8
ML_DQK = 64
ML_DV = 128
LRU_HEADS = 4
LRU_BW = 256
CONV_W = 4
LRU_C = 8.0
N_EXPERTS = 8

LANES = 128
SUBLANES = 8
VMEM_LIMIT = 56 * 1024 * 1024

ML_CHUNK = 256
LRU_TILE = 256
PROJ_TM = 1024
FFN_TM = 1024
MOE_TM = 512
FFN_TF = 512
ROUTER_TM = 1024
GATEPREP_ROWS = 1024
MOE_META_T = 512
MOE_META_W = 4096
MOE_DMA_T = 1024


def _cparams(sem):
    return pltpu.CompilerParams(dimension_semantics=sem, vmem_limit_bytes=VMEM_LIMIT)


def _norm_mod(x, mod_ref, row):
    r = lax.rsqrt(jnp.mean(x * x, axis=-1, keepdims=True) + RMS_EPS)
    return (x * r) * mod_ref[row + 1:row + 2, :] + mod_ref[row:row + 1, :]


def _sigmoid(x):
    return 1.0 / (1.0 + jnp.exp(-x))


def _log_sigmoid(x):
    return jnp.minimum(x, 0.0) - jnp.log1p(jnp.exp(-jnp.abs(x)))


def _ada_kernel(c_ref, w_ref, b_ref, mul_ref, add_ref, o_ref):
    c = c_ref[...]
    ca = (c * _sigmoid(c)).astype(BF16)
    mod = jnp.dot(ca, w_ref[...].astype(BF16), preferred_element_type=F32) + b_ref[...]
    o_ref[...] = mul_ref[...] * mod + add_ref[...]


def _ada(c, ada_w, ada_b, mul, add):
    depth, d, _ = ada_w.shape
    b = c.shape[0]
    row = pl.BlockSpec((None, None, 1, d), lambda i, j: (i, j, 0, 0))
    out = pl.pallas_call(
        _ada_kernel,
        grid=(depth, 6),
        in_specs=[pl.BlockSpec((b, d), lambda i, j: (0, 0)),
                  pl.BlockSpec((None, d, d), lambda i, j: (i, 0, j)),
                  row, row, row],
        out_specs=pl.BlockSpec((None, None, b, d), lambda i, j: (i, j, 0, 0)),
        out_shape=jax.ShapeDtypeStruct((depth, 6, b, d), F32),
        compiler_params=_cparams(("arbitrary", "arbitrary")),
        name="ada_mod",
    )(c, ada_w, ada_b.reshape(depth, 6, 1, d), mul, add)
    return out.transpose(0, 2, 1, 3)


def _mlproj_kernel(x_ref, mod_ref, wq_ref, wkt_ref, wvo_ref, wg_ref, bg_ref,
                   q_ref, kt_ref, vo_ref, g_ref):
    h = _norm_mod(x_ref[...], mod_ref, 0).astype(BF16)
    q = jnp.dot(h, wq_ref[...], preferred_element_type=F32)
    q_ref[...] = (q * (ML_DQK ** -0.5)).astype(BF16)
    kt = lax.dot_general(wkt_ref[...], h, (((1,), (1,)), ((), ())), preferred_element_type=F32)
    kt_ref[...] = kt.astype(BF16)
    vo_ref[...] = jnp.dot(h, wvo_ref[...], preferred_element_type=F32).astype(BF16)
    g_ref[...] = jnp.dot(h, wg_ref[...], preferred_element_type=F32) + bg_ref[...]


def _mlproj(x2, mod, wq, wkt, wvo, wg, bg, seq):
    m, d = x2.shape
    tm = min(PROJ_TM, seq)
    tpb = seq // tm
    nq, nvo = wq.shape[1], wvo.shape[1]
    full = lambda shape: pl.BlockSpec(shape, lambda i: (0,) * len(shape))
    return pl.pallas_call(
        _mlproj_kernel,
        grid=(m // tm,),
        in_specs=[pl.BlockSpec((tm, d), lambda i: (i, 0)),
                  pl.BlockSpec((None, 6, d), lambda i: (i // tpb, 0, 0)),
                  full(wq.shape), full(wkt.shape), full(wvo.shape), full(wg.shape), full(bg.shape)],
        out_specs=[pl.BlockSpec((tm, nq), lambda i: (i, 0)),
                   pl.BlockSpec((nq, tm), lambda i: (0, i)),
                   pl.BlockSpec((tm, nvo), lambda i: (i, 0)),
                   pl.BlockSpec((tm, LANES), lambda i: (i, 0))],
        out_shape=[jax.ShapeDtypeStruct((m, nq), BF16),
                   jax.ShapeDtypeStruct((nq, m), BF16),
                   jax.ShapeDtypeStruct((m, nvo), BF16),
                   jax.ShapeDtypeStruct((m, LANES), F32)],
        compiler_params=_cparams(("parallel",)),
        name="mlstm_proj",
    )(x2, mod, wq, wkt, wvo, wg, bg)


def _gateprep_kernel(g_ref, o_ref, *, chunk):
    n = chunk
    r = lax.broadcasted_iota(jnp.int32, (n, n), 0)
    s = lax.broadcasted_iota(jnp.int32, (n, n), 1)
    tril = jnp.where(s <= r, 1.0, 0.0).astype(F32)
    lane = lax.broadcasted_iota(jnp.int32, (n, LANES), 1)
    for c in range(g_ref.shape[0] // chunk):
        g = g_ref[c * chunk:(c + 1) * chunk, :]
        b = jnp.dot(tril, _log_sigmoid(g), precision=HIGHEST, preferred_element_type=F32)
        b_i = pltpu.roll(b, LANES - ML_HEADS, axis=1)
        o_ref[c * chunk:(c + 1) * chunk, :] = jnp.where(lane < ML_HEADS, g - b_i, b)


def _gateprep(gates, chunk, seq):
    m = gates.shape[0]
    rows = min(GATEPREP_ROWS, seq)
    return pl.pallas_call(
        functools.partial(_gateprep_kernel, chunk=chunk),
        grid=(m // rows,),
        in_specs=[pl.BlockSpec((rows, LANES), lambda i: (i, 0))],
        out_specs=pl.BlockSpec((rows, LANES), lambda i: (i, 0)),
        out_shape=jax.ShapeDtypeStruct((m, LANES), F32),
        compiler_params=_cparams(("parallel",)),
        name="mlstm_gateprep",
    )(gates)


def _mlstm_kernel(x_ref, mod_ref, q_ref, kt_ref, vo_ref, arow_ref, col_ref, hg_ref, wout_ref,
                  o_ref, c_scr, m_scr, hs_scr):
    L = q_ref.shape[0]
    vw = ML_HEADS * ML_DV

    @pl.when(pl.program_id(1) == 0)
    def _():
        c_scr[...] = jnp.zeros_like(c_scr)
        m_scr[...] = jnp.zeros_like(m_scr)

    t_idx = lax.broadcasted_iota(jnp.int32, (L, L), 0)
    s_idx = lax.broadcasted_iota(jnp.int32, (L, L), 1)
    causal = s_idx <= t_idx
    lane_q = lax.broadcasted_iota(jnp.int32, (1, 2 * ML_DQK), 1)
    head_mask = [jnp.where(lane_q < ML_DQK, 1.0, 0.0).astype(BF16),
                 jnp.where(lane_q >= ML_DQK, 1.0, 0.0).astype(BF16)]
    lane_v = lax.broadcasted_iota(jnp.int32, (L, ML_DV), 1)
    ones_blk = jnp.ones((L, ML_DV), BF16)

    heads = range(ML_HEADS)
    m_all = [m_scr[h] for h in heads]
    c_all = [c_scr[h] for h in heads]
    a_rows = [arow_ref[h:h + 1, :] for h in heads]
    b_cols = [col_ref[:, ML_HEADS + h:ML_HEADS + h + 1] for h in heads]

    a_mats = [jnp.where(causal, a_rows[h], -jnp.inf) for h in heads]
    m_cols = [jnp.maximum(m_all[h], jnp.max(a_mats[h], axis=1, keepdims=True)) for h in heads]
    ws = [jnp.exp(a_mats[h] - m_cols[h]) for h in heads]
    e_cols = [jnp.exp(m_all[h] - m_cols[h]) for h in heads]
    floors = [jnp.exp(-(b_cols[h] + m_cols[h])) for h in heads]

    v_augs, ss, inters = [], [], []
    for h in heads:
        p, half = h // 2, h % 2
        qm = q_ref[:, p * 128:(p + 1) * 128] * head_mask[half]
        kt2 = kt_ref[p * 128:(p + 1) * 128, :]
        v_augs.append(jnp.concatenate([vo_ref[:, h * ML_DV:(h + 1) * ML_DV], ones_blk], axis=1))
        ss.append((jnp.dot(qm, kt2, preferred_element_type=F32) * ws[h]).astype(BF16))
        c_pair = jnp.concatenate([c_all[h], c_all[h]], axis=0).astype(BF16)
        inters.append(jnp.dot(qm, c_pair, preferred_element_type=F32))

    hhs = []
    for h in heads:
        num_aug = jnp.dot(ss[h], v_augs[h], preferred_element_type=F32) + e_cols[h] * inters[h]
        hhs.append(num_aug[:, :ML_DV] / jnp.maximum(jnp.abs(num_aug[:, ML_DV:2 * ML_DV]), floors[h]))

    rs = [lax.rsqrt(jnp.mean(hhs[h] * hhs[h], axis=-1, keepdims=True) + RMS_EPS) for h in heads]
    hs_all = []
    for h in heads:
        og = vo_ref[:, vw + h * ML_DV: vw + (h + 1) * ML_DV]
        hn = hhs[h] * rs[h] * hg_ref[:, h * ML_DV:(h + 1) * ML_DV] * _sigmoid(og.astype(F32))
        hs_all.append(hn.astype(BF16))

    c_new, m_new = [], []
    for h in heads:
        m_last = m_cols[h][L - 1:L, :]
        w_s = jnp.exp(a_rows[h] - m_last)
        decay = jnp.exp(m_all[h] - m_last)
        kw = (kt_ref[h * ML_DQK:(h + 1) * ML_DQK, :].astype(F32) * w_s).astype(BF16)
        c_new.append(decay * c_all[h] + jnp.dot(kw, v_augs[h], preferred_element_type=F32))
        m_new.append(b_cols[h][L - 1:L, :] + m_last)

    for h in range(ML_HEADS):
        c_scr[h] = c_new[h]
        m_scr[h] = m_new[h]
    mix = jnp.dot(jnp.concatenate(hs_all, axis=1), wout_ref[...], preferred_element_type=F32)
    o_ref[...] = x_ref[...] + mod_ref[2:3, :] * mix


def _mlstm(x2, mod, q, kt, vo, a_rows, cols, head_g, w_out, batch, seq):
    m, d = x2.shape
    L = min(ML_CHUNK, seq)
    nc = seq // L
    nq, nvo = q.shape[1], vo.shape[1]
    row_blk = lambda w: pl.BlockSpec((L, w), lambda b, c: (b * nc + c, 0))
    return pl.pallas_call(
        _mlstm_kernel,
        grid=(batch, nc),
        in_specs=[row_blk(d),
                  pl.BlockSpec((None, 6, d), lambda b, c: (b, 0, 0)),
                  row_blk(nq),
                  pl.BlockSpec((nq, L), lambda b, c: (0, b * nc + c)),
                  row_blk(nvo),
                  pl.BlockSpec((ML_HEADS, L), lambda b, c: (0, b * nc + c)),
                  row_blk(LANES),
                  pl.BlockSpec((1, d), lambda b, c: (0, 0)),
                  pl.BlockSpec((d, d), lambda b, c: (0, 0))],
        out_specs=row_blk(d),
        out_shape=jax.ShapeDtypeStruct((m, d), F32),
        scratch_shapes=[pltpu.VMEM((ML_HEADS, ML_DQK, 2 * ML_DV), F32),
                        pltpu.VMEM((ML_HEADS, 1, 1), F32),
                        pltpu.VMEM((L, d), BF16)],
        compiler_params=_cparams(("parallel", "arbitrary")),
        name="mlstm_core",
    )(x2, mod, q, kt, vo, a_rows, cols, head_g, w_out)


def _lru_kernel(x_ref, mod_ref, gx_ref, cw_ref, cb_ref, wgx_ref, wga_ref, bgx_ref, bga_ref, ap_ref,
                wout_ref, o_ref, xbuf, a_scr, u_scr, h_scr, hc_scr):
    T, W = h_scr.shape
    PAD = SUBLANES

    @pl.when(pl.program_id(1) == 0)
    def _():
        xbuf[0:PAD, :] = jnp.zeros((PAD, W), F32)
        hc_scr[...] = jnp.zeros_like(hc_scr)

    xbuf[PAD:PAD + T, :] = gx_ref[:, W:2 * W].astype(F32)
    xc = cb_ref[...] + xbuf[PAD - 3:PAD - 3 + T, :] * cw_ref[0:1, :]
    for j in range(1, CONV_W):
        xc = xc + xbuf[PAD - 3 + j:PAD - 3 + j + T, :] * cw_ref[j:j + 1, :]
    xbuf[0:PAD, :] = xbuf[T:T + PAD, :]

    xcb = xc.astype(BF16)
    sp = ap_ref[...]
    sp = jnp.maximum(-sp, 0.0) + jnp.log1p(jnp.exp(-jnp.abs(sp)))
    for hd in range(LRU_HEADS):
        sl = slice(hd * LRU_BW, (hd + 1) * LRU_BW)
        xh = xcb[:, sl]
        gxh = _sigmoid(jnp.dot(xh, wgx_ref[hd], preferred_element_type=F32) + bgx_ref[:, sl])
        gah = _sigmoid(jnp.dot(xh, wga_ref[hd], preferred_element_type=F32) + bga_ref[:, sl])
        log_a = (-LRU_C) * gah * sp[:, sl]
        a = jnp.exp(log_a)
        a_scr[:, sl] = a
        u_scr[:, sl] = xc[:, sl] * gxh * jnp.sqrt(jnp.tanh(-log_a) * (1.0 + a * a))

    row = lax.broadcasted_iota(jnp.int32, (SUBLANES, W), 0)

    first = row == 0

    def body(r, hc):
        base = pl.multiple_of(r * SUBLANES, SUBLANES)
        a8 = a_scr[pl.ds(base, SUBLANES), :]
        u8 = u_scr[pl.ds(base, SUBLANES), :]
        u8 = u8 + jnp.where(first, a8 * hc, 0.0)
        a8 = jnp.where(first, 0.0, a8)
        for sft in (1, 2, 4):
            u8 = a8 * pltpu.roll(u8, sft, axis=0) + u8
            if sft < 4:
                a8 = a8 * pltpu.roll(a8, sft, axis=0)
        h_scr[pl.ds(base, SUBLANES), :] = u8
        return u8[SUBLANES - 1:SUBLANES, :]

    hc_scr[...] = lax.fori_loop(0, T // SUBLANES, body, hc_scr[...], unroll=2)

    gb = gx_ref[:, 0:W].astype(F32)
    y = (h_scr[...] * jax.nn.gelu(gb)).astype(BF16)
    mix = jnp.dot(y, wout_ref[...], preferred_element_type=F32)
    o_ref[...] = x_ref[...] + mod_ref[2:3, :] * mix


def _lru(x2, mod, gx, conv_w, conv_b, wgx, wga, bgx, bga, a_param, w_out, batch, seq):
    m, d = x2.shape
    w = w_out.shape[0]
    T = min(LRU_TILE, seq)
    nt = seq // T
    row_blk = lambda wd: pl.BlockSpec((T, wd), lambda b, t: (b * nt + t, 0))
    full = lambda a: pl.BlockSpec(a.shape, lambda b, t: (0,) * a.ndim)
    return pl.pallas_call(
        _lru_kernel,
        grid=(batch, nt),
        in_specs=[row_blk(d),
                  pl.BlockSpec((None, 6, d), lambda b, t: (b, 0, 0)),
                  row_blk(2 * w),
                  full(conv_w), full(conv_b), full(wgx), full(wga), full(bgx), full(bga), full(a_param),
                  full(w_out)],
        out_specs=row_blk(d),
        out_shape=jax.ShapeDtypeStruct((m, d), F32),
        scratch_shapes=[pltpu.VMEM((T + SUBLANES, w), F32),
                        pltpu.VMEM((T, w), F32),
                        pltpu.VMEM((T, w), F32),
                        pltpu.VMEM((T, w), F32),
                        pltpu.VMEM((1, w), F32)],
        compiler_params=_cparams(("parallel", "arbitrary")),
        name="rglru_core",
    )(x2, mod, gx, conv_w, conv_b, wgx, wga, bgx, bga, a_param, w_out)


def _lruproj_kernel(x_ref, mod_ref, w_ref, o_ref):
    h = _norm_mod(x_ref[...], mod_ref, 0).astype(BF16)
    o_ref[...] = jnp.dot(h, w_ref[...], preferred_element_type=F32).astype(BF16)


def _lruproj(x2, mod, w, seq):
    m, d = x2.shape
    n = w.shape[1]
    tm = min(PROJ_TM, seq)
    tpb = seq // tm
    return pl.pallas_call(
        _lruproj_kernel,
        grid=(m // tm,),
        in_specs=[pl.BlockSpec((tm, d), lambda i: (i, 0)),
                  pl.BlockSpec((None, 6, d), lambda i: (i // tpb, 0, 0)),
                  pl.BlockSpec((d, n), lambda i: (0, 0))],
        out_specs=pl.BlockSpec((tm, n), lambda i: (i, 0)),
        out_shape=jax.ShapeDtypeStruct((m, n), BF16),
        compiler_params=_cparams(("parallel",)),
        name="rglru_proj",
    )(x2, mod, w)


def _final_norm(x, g_ref):
    return x * lax.rsqrt(jnp.mean(x * x, axis=-1, keepdims=True) + RMS_EPS) * g_ref[...]


def _swiglu(h, w13_ref, w2_ref):
    dff = w2_ref.shape[0]
    acc = None
    for c in range(dff // FFN_TF):
        lo, hi = c * FFN_TF, (c + 1) * FFN_TF
        a = jnp.dot(h, w13_ref[:, lo:hi], preferred_element_type=F32)
        b = jnp.dot(h, w13_ref[:, dff + lo:dff + hi], preferred_element_type=F32)
        g = (a * _sigmoid(a) * b).astype(BF16)
        y = jnp.dot(g, w2_ref[lo:hi, :], preferred_element_type=F32)
        acc = y if acc is None else acc + y
    return acc


def _ffn_kernel(x_ref, mod_ref, w13_ref, w2_ref, o_ref):
    x = x_ref[...]
    h = _norm_mod(x, mod_ref, 3).astype(BF16)
    o_ref[...] = x + mod_ref[5:6, :] * _swiglu(h, w13_ref, w2_ref)


def _ffn(x2, mod, w13, w2, layer, seq):
    m, d = x2.shape
    dff = w2.shape[1]
    tm = min(FFN_TM, seq)
    tpb = seq // tm
    return pl.pallas_call(
        _ffn_kernel,
        grid=(m // tm,),
        in_specs=[pl.BlockSpec((tm, d), lambda i: (i, 0)),
                  pl.BlockSpec((None, 6, d), lambda i: (i // tpb, 0, 0)),
                  pl.BlockSpec((None, d, 2 * dff), lambda i: (layer, 0, 0), pipeline_mode=pl.Buffered(1)),
                  pl.BlockSpec((None, dff, d), lambda i: (layer, 0, 0), pipeline_mode=pl.Buffered(1))],
        out_specs=pl.BlockSpec((tm, d), lambda i: (i, 0)),
        out_shape=jax.ShapeDtypeStruct((m, d), F32),
        compiler_params=_cparams(("parallel",)),
        name="ffn_swiglu",
    )(x2, mod, w13, w2)


def _router_kernel(x_ref, mod_ref, rw_ref, rwh_ref, rb_ref, info_ref, info_t_ref):
    h = _norm_mod(x_ref[...], mod_ref, 3)
    h_hi = h.astype(BF16)
    h_lo = (h - h_hi.astype(F32)).astype(BF16)
    p_hi = jnp.dot(h_hi, rw_ref[...], preferred_element_type=F32)
    p_lo = jnp.dot(h_lo, rwh_ref[...], preferred_element_type=F32)
    logits = p_hi + pltpu.roll(p_hi, LANES - N_EXPERTS, axis=1) + p_lo + rb_ref[...]
    lt = logits.T[0:N_EXPERTS, :]
    tm = lt.shape[1]
    sub = lax.broadcasted_iota(jnp.int32, lt.shape, 0).astype(F32)
    v1 = jnp.max(lt, axis=0, keepdims=True)
    i1 = jnp.min(jnp.where(lt == v1, sub, float(N_EXPERTS)), axis=0, keepdims=True)
    rest = jnp.where(sub == i1, -jnp.inf, lt)
    v2 = jnp.max(rest, axis=0, keepdims=True)
    i2 = jnp.min(jnp.where(rest == v2, sub, float(N_EXPERTS)), axis=0, keepdims=True)
    e2 = jnp.exp(v2 - v1)
    w1 = 1.0 / (1.0 + e2)
    w2 = e2 / (1.0 + e2)
    info_t = jnp.where(sub == 0.0, w1, jnp.where(sub == 1.0, w2, jnp.where(sub == 2.0, i1,
                                                                          jnp.where(sub == 3.0, i2, 0.0))))
    info_t_ref[...] = info_t
    info_ref[...] = jnp.concatenate([info_t, jnp.zeros((LANES - N_EXPERTS, tm), F32)], axis=0).T


def _router(x2, mod, router_w, router_b, seq):
    m, d = x2.shape
    tm = min(ROUTER_TM, seq)
    tpb = seq // tm
    ne = router_w.shape[1]
    w_hi = router_w.astype(BF16)
    w_lo = (router_w - w_hi.astype(F32)).astype(BF16)
    rw = jnp.pad(jnp.concatenate([w_hi, w_lo], axis=1), ((0, 0), (0, LANES - 2 * ne)))
    rwh = jnp.pad(w_hi, ((0, 0), (0, LANES - ne)))
    rb = jnp.pad(router_b, (0, LANES - ne), constant_values=-jnp.inf).reshape(1, LANES)
    return pl.pallas_call(
        _router_kernel,
        grid=(m // tm,),
        in_specs=[pl.BlockSpec((tm, d), lambda i: (i, 0)),
                  pl.BlockSpec((None, 6, d), lambda i: (i // tpb, 0, 0)),
                  pl.BlockSpec((d, LANES), lambda i: (0, 0)),
                  pl.BlockSpec((d, LANES), lambda i: (0, 0)),
                  pl.BlockSpec((1, LANES), lambda i: (0, 0))],
        out_specs=[pl.BlockSpec((tm, LANES), lambda i: (i, 0)),
                   pl.BlockSpec((SUBLANES, tm), lambda i: (0, i))],
        out_shape=[jax.ShapeDtypeStruct((m, LANES), F32),
                   jax.ShapeDtypeStruct((SUBLANES, m), F32)],
        compiler_params=_cparams(("parallel",)),
        name="moe_router",
    )(x2, mod, rw, rwh, rb)


def _moe_meta_kernel(sel_ref, tri_ref, pos_ref, meta_ref, run1, run2, off1, off2, *, tm):
    p = pl.program_id(0)
    i = pl.program_id(1)
    T = tri_ref.shape[0]

    @pl.when((p == 0) & (i == 0))
    def _():
        run1[...] = jnp.zeros_like(run1)
        run2[...] = jnp.zeros_like(run2)

    @pl.when((p == 1) & (i == 0))
    def _():
        cnt1 = run1[...]
        tot = cnt1 + run2[...]
        padded = jnp.floor((tot + (tm - 1)) * (1.0 / tm)) * tm
        e_idx = lax.broadcasted_iota(jnp.int32, padded.shape, 0)
        off = jnp.zeros_like(padded)
        for e in range(N_EXPERTS - 1):
            off = off + jnp.where(e_idx > e, padded[e:e + 1, :], 0.0)
        off1[...] = off
        off2[...] = off + cnt1
        meta_ref[0:SUBLANES, :] = off
        meta_ref[SUBLANES:2 * SUBLANES, :] = padded
        meta_ref[2 * SUBLANES:3 * SUBLANES, :] = tot
        run1[...] = jnp.zeros_like(run1)
        run2[...] = jnp.zeros_like(run2)

    row = lax.broadcasted_iota(jnp.int32, (SUBLANES, T), 0)
    sub = row.astype(F32)
    for sb in range(sel_ref.shape[1] // T):
        cols = slice(sb * T, (sb + 1) * T)
        oh1 = jnp.where(sel_ref[2:3, cols] == sub, 1.0, 0.0)
        oh2 = jnp.where(sel_ref[3:4, cols] == sub, 1.0, 0.0)

        @pl.when(p == 1)
        def _():
            tri = tri_ref[...]
            pre1 = jnp.dot(oh1.astype(BF16), tri, preferred_element_type=F32)
            pre2 = jnp.dot(oh2.astype(BF16), tri, preferred_element_type=F32)
            pos1 = jnp.sum(oh1 * (pre1 + (off1[:, 0:1] + run1[:, 0:1])), axis=0, keepdims=True)
            pos2 = jnp.sum(oh2 * (pre2 + (off2[:, 0:1] + run2[:, 0:1])), axis=0, keepdims=True)
            pos = jnp.where(row == 0, pos1, jnp.where(row == 1, pos2, 0.0))
            pos_ref[:, cols] = pos.astype(jnp.int32)

        run1[...] += jnp.sum(oh1, axis=1, keepdims=True)
        run2[...] += jnp.sum(oh2, axis=1, keepdims=True)


def _moe_meta(info_t, tm):
    m = info_t.shape[1]
    T = min(MOE_META_T, m)
    W = min(MOE_META_W, m)
    s_i = lax.broadcasted_iota(jnp.int32, (T, T), 0)
    t_i = lax.broadcasted_iota(jnp.int32, (T, T), 1)
    tri = (s_i < t_i).astype(BF16)
    return pl.pallas_call(
        functools.partial(_moe_meta_kernel, tm=tm),
        grid=(2, m // W),
        in_specs=[pl.BlockSpec((SUBLANES, W), lambda p, i: (0, i)),
                  pl.BlockSpec((T, T), lambda p, i: (0, 0))],
        out_specs=[pl.BlockSpec((SUBLANES, W), lambda p, i: (0, i * p)),
                   pl.BlockSpec((3 * SUBLANES, LANES), lambda p, i: (0, 0))],
        out_shape=[jax.ShapeDtypeStruct((SUBLANES, m), jnp.int32),
                   jax.ShapeDtypeStruct((3 * SUBLANES, LANES), F32)],
        scratch_shapes=[pltpu.VMEM((SUBLANES, LANES), F32)] * 4,
        compiler_params=_cparams(("arbitrary", "arbitrary")),
        name="moe_meta",
    )(info_t, tri)


def _row_copy(src_ref, src_row, dst_ref, dst_row, sem):
    return pltpu.make_async_copy(src_ref.at[pl.ds(src_row, 1), :], dst_ref.at[pl.ds(dst_row, 1), :], sem)


def _moe_dispatch_kernel(fill_ref, pos1_ref, pos2_ref, x_ref, mod_ref, hg_ref, h_scr, z_scr, sem, zsem, *, tm):
    T = x_ref.shape[0]
    ne = N_EXPERTS

    def zero_row(e, r):
        return _row_copy(z_scr, 0, hg_ref, fill_ref[e] + r, zsem)

    def zero_tile(t):
        row0 = pl.multiple_of((fill_ref[2 * ne] + t) * tm, tm)
        return pltpu.make_async_copy(z_scr, hg_ref.at[pl.ds(row0, tm), :], zsem)

    def for_each_fill(fn):
        for e in range(ne):
            lax.fori_loop(0, fill_ref[ne + e], lambda r, c, e=e: (fn(zero_row(e, r)), c)[1], 0)
        lax.fori_loop(0, fill_ref[2 * ne + 1], lambda t, c: (fn(zero_tile(t)), c)[1], 0)

    @pl.when(pl.program_id(0) == 0)
    def _():
        z_scr[...] = jnp.zeros_like(z_scr)
        for_each_fill(lambda cp: cp.start())
        for_each_fill(lambda cp: cp.wait())

    h_scr[...] = _norm_mod(x_ref[...], mod_ref, 3)

    def body(j, carry):
        _row_copy(h_scr, j, hg_ref, pos1_ref[j], sem).start()
        _row_copy(h_scr, j, hg_ref, pos2_ref[j], sem).start()
        return carry

    lax.fori_loop(0, T, body, 0, unroll=32)
    pltpu.make_async_copy(h_scr, hg_ref.at[pl.ds(0, T), :], sem).wait()
    pltpu.make_async_copy(h_scr, hg_ref.at[pl.ds(0, T), :], sem).wait()


def _moe_dispatch(fill, pos1, pos2, x2, mod, n_rows, tm, seq):
    m, d = x2.shape
    T = min(MOE_DMA_T, seq)
    tpb = seq // T
    smem = lambda: pl.BlockSpec((T,), lambda i, fill: (i,), memory_space=pltpu.SMEM)
    grid_spec = pltpu.PrefetchScalarGridSpec(
        num_scalar_prefetch=1,
        grid=(m // T,),
        in_specs=[smem(), smem(),
                  pl.BlockSpec((T, d), lambda i, fill: (i, 0)),
                  pl.BlockSpec((None, 6, d), lambda i, fill: (i // tpb, 0, 0))],
        out_specs=pl.BlockSpec(memory_space=pl.ANY),
        scratch_shapes=[pltpu.VMEM((T, d), F32), pltpu.VMEM((tm, d), F32),
                        pltpu.SemaphoreType.DMA(()), pltpu.SemaphoreType.DMA(())])
    return pl.pallas_call(
        functools.partial(_moe_dispatch_kernel, tm=tm),
        grid_spec=grid_spec,
        out_shape=jax.ShapeDtypeStruct((n_rows, d), F32),
        compiler_params=_cparams(("arbitrary",)),
        name="moe_dispatch",
    )(fill, pos1, pos2, x2, mod)


def _moe_group_kernel(eid_ref, nv_ref, hg_ref, w13_ref, w2_ref, o_ref):
    del eid_ref
    r = pl.program_id(0)

    @pl.when(r < nv_ref[0])
    def _():
        o_ref[...] = _swiglu(hg_ref[...].astype(BF16), w13_ref, w2_ref)

    @pl.when(r >= nv_ref[0])
    def _():
        o_ref[...] = jnp.zeros_like(o_ref)


def _moe_group(eid, nvalid, hg, w13, w2, layer, tm):
    n_rows, d = hg.shape
    dff = w2.shape[2]
    grid_spec = pltpu.PrefetchScalarGridSpec(
        num_scalar_prefetch=2,
        grid=(n_rows // tm,),
        in_specs=[pl.BlockSpec((tm, d), lambda r, eid, nv: (jnp.minimum(r, nv[0] - 1), 0)),
                  pl.BlockSpec((None, None, d, 2 * dff), lambda r, eid, nv: (layer, eid[r], 0, 0),
                               pipeline_mode=pl.Buffered(1)),
                  pl.BlockSpec((None, None, dff, d), lambda r, eid, nv: (layer, eid[r], 0, 0),
                               pipeline_mode=pl.Buffered(1))],
        out_specs=pl.BlockSpec((tm, d), lambda r, eid, nv: (r, 0)))
    return pl.pallas_call(
        _moe_group_kernel,
        grid_spec=grid_spec,
        out_shape=jax.ShapeDtypeStruct((n_rows, d), F32),
        compiler_params=_cparams(("arbitrary",)),
        name="moe_experts",
    )(eid, nvalid, hg, w13, w2)


def _moe_combine_kernel(pos1_ref, pos2_ref, x_ref, mod_ref, info_ref, fg_ref, ys_ref, o_ref, y1, y2, sem,
                        *, final):
    T = x_ref.shape[0]

    def body(j, carry):
        _row_copy(ys_ref, pos1_ref[j], y1, j, sem).start()
        _row_copy(ys_ref, pos2_ref[j], y2, j, sem).start()
        return carry

    lax.fori_loop(0, T, body, 0, unroll=32)
    pltpu.make_async_copy(ys_ref.at[pl.ds(0, T), :], y1, sem).wait()
    pltpu.make_async_copy(ys_ref.at[pl.ds(0, T), :], y2, sem).wait()
    info = info_ref[...]
    y = info[:, 0:1] * y1[...] + info[:, 1:2] * y2[...]
    out = x_ref[...] + mod_ref[5:6, :] * y
    o_ref[...] = _final_norm(out, fg_ref) if final else out


def _moe_combine(pos1, pos2, x2, mod, info, final_g, ys, final, seq):
    m, d = x2.shape
    T = min(MOE_DMA_T, seq)
    tpb = seq // T
    smem = lambda: pl.BlockSpec((T,), lambda i: (i,), memory_space=pltpu.SMEM)
    return pl.pallas_call(
        functools.partial(_moe_combine_kernel, final=final),
        grid=(m // T,),
        in_specs=[smem(), smem(),
                  pl.BlockSpec((T, d), lambda i: (i, 0)),
                  pl.BlockSpec((None, 6, d), lambda i: (i // tpb, 0, 0)),
                  pl.BlockSpec((T, LANES), lambda i: (i, 0)),
                  pl.BlockSpec((1, d), lambda i: (0, 0)),
                  pl.BlockSpec(memory_space=pl.ANY)],
        out_specs=pl.BlockSpec((T, d), lambda i: (i, 0)),
        out_shape=jax.ShapeDtypeStruct((m, d), F32),
        scratch_shapes=[pltpu.VMEM((T, d), F32), pltpu.VMEM((T, d), F32), pltpu.SemaphoreType.DMA(())],
        compiler_params=_cparams(("arbitrary",)),
        name="moe_combine",
    )(pos1, pos2, x2, mod, info, final_g, ys)


def _moe(x2, mod, rw, rb, w13, w2, layer, final_g, final, seq):
    m, d = x2.shape
    ne = w2.shape[1]
    tm = min(MOE_TM, seq)
    n_rows = 2 * m + ne * tm
    info, info_t = _router(x2, mod, rw, rb, seq)
    pos, meta = _moe_meta(info_t, tm)
    meta = meta[:, 0].astype(jnp.int32)
    off, padded, tot = (meta[k * SUBLANES:k * SUBLANES + ne] for k in range(3))
    ends = off + padded
    nvalid = ends[ne - 1] // tm
    starts = jnp.minimum(jnp.arange(n_rows // tm, dtype=jnp.int32), nvalid - 1) * tm
    eid = jnp.sum((starts[:, None] >= ends[None, :]).astype(jnp.int32), axis=1)
    fill = jnp.concatenate([off + tot, padded - tot, jnp.stack([nvalid, n_rows // tm - nvalid])])
    hg = _moe_dispatch(fill, pos[0], pos[1], x2, mod, n_rows, tm, seq)
    ys = _moe_group(eid, nvalid.reshape(1), hg, w13, w2, layer, tm)
    return _moe_combine(pos[0], pos[1], x2, mod, info, final_g, ys, final, seq)


def kernel(x, c, ada_w, ada_b, norm_g, ml_w_in, ml_gate_b, ml_head_g, ml_w_out, ffn_w13, ffn_w2,
           lru_w_in, lru_conv_w, lru_conv_b, lru_gate_w, lru_gate_b, lru_a_param, lru_w_out,
           moe_router_w, moe_router_b, moe_w13, moe_w2, final_norm_g):
    batch, seq, d = x.shape
    depth = ada_w.shape[0]
    m = batch * seq
    qkw = ML_HEADS * ML_DQK
    vw = ML_HEADS * ML_DV

    ones = jnp.ones((depth, 1, d), F32)
    zeros = jnp.zeros((depth, 1, d), F32)
    g0, g1 = norm_g[:, 0:1, :], norm_g[:, 1:2, :]
    mul = jnp.concatenate([ones, g0, ones, ones, g1, ones], axis=1).reshape(depth, 6, 1, d)
    add = jnp.concatenate([zeros, g0, zeros, zeros, g1, zeros], axis=1).reshape(depth, 6, 1, d)
    mods = _ada(c, ada_w, ada_b, mul, add)

    ffn_w13_b, ffn_w2_b = ffn_w13.astype(BF16), ffn_w2.astype(BF16)
    moe_w13_b, moe_w2_b = moe_w13.astype(BF16), moe_w2.astype(BF16)

    x2 = x.reshape(m, d)
    for i in range(depth):
        j = i // 2
        mod = mods[i]
        last = i == depth - 1
        if i % 2 == 0:
            w_in = ml_w_in[j]
            wq = w_in[:, :qkw].astype(BF16)
            wkt = w_in[:, qkw:2 * qkw].T.astype(BF16)
            wvo = w_in[:, 2 * qkw:2 * qkw + 2 * vw].astype(BF16)
            wg = jnp.pad(w_in[:, 2 * qkw + 2 * vw:], ((0, 0), (0, LANES - 2 * ML_HEADS))).astype(BF16)
            bg = jnp.pad(ml_gate_b[j], (0, LANES - 2 * ML_HEADS)).reshape(1, LANES)
            q, kt, vo, gates = _mlproj(x2, mod, wq, wkt, wvo, wg, bg, seq)
            cols = _gateprep(gates, min(ML_CHUNK, seq), seq)
            a_rows = cols[:, :ML_HEADS].T
            x2 = _mlstm(x2, mod, q, kt, vo, a_rows, cols, ml_head_g[j].reshape(1, vw),
                        ml_w_out[j].astype(BF16), batch, seq)
            x2 = _ffn(x2, mod, ffn_w13_b, ffn_w2_b, j, seq)
        else:
            gx = _lruproj(x2, mod, lru_w_in[j].astype(BF16), seq)
            gw = lru_gate_w[j]
            gb = lru_gate_b[j]
            wgx = gw[:, :, :LRU_BW].astype(BF16)
            wga = gw[:, :, LRU_BW:].astype(BF16)
            bgx = gb[:, :LRU_BW].reshape(1, -1)
            bga = gb[:, LRU_BW:].reshape(1, -1)
            x2 = _lru(x2, mod, gx, lru_conv_w[j], lru_conv_b[j].reshape(1, -1), wgx, wga, bgx, bga,
                      lru_a_param[j].reshape(1, -1), lru_w_out[j].astype(BF16), batch, seq)
            x2 = _moe(x2, mod, moe_router_w[j], moe_router_b[j], moe_w13_b, moe_w2_b, j, final_norm_g.reshape(1, d), last, seq)
    return x2.reshape(batch, seq, d)
```

```python
import functools

import jax
import jax.numpy as jnp
from jax import lax
from jax.experimental import pallas as pl
from jax.experimental.pallas import tpu as pltpu

F32 = jnp.float32
BF16 = jnp.bfloat16
HIGHEST = lax.Precision.HIGHEST

RMS_EPS = 1e-6
ML_HEADS = 8
ML_DQK = 64
ML_DV = 128
LRU_HEADS = 4
LRU_BW = 256
CONV_W = 4
LRU_C = 8.0
N_EXPERTS = 8

LANES = 128
SUBLANES = 8
VMEM_LIMIT = 56 * 1024 * 1024

ML_CHUNK = 256
LRU_TILE = 512
PROJ_TM = 1024
FFN_TM = 1024
MOE_TM = 512
FFN_TF = 512
ROUTER_TM = 2048
GATEPREP_ROWS = 4096
MOE_META_T = 512
MOE_META_W = 4096
MOE_DMA_T = 1024
MOE_DISPATCH_T = 2048


def _cparams(sem):
    return pltpu.CompilerParams(dimension_semantics=sem, vmem_limit_bytes=VMEM_LIMIT)


def _norm_mod(x, mod_ref, row):
    r = lax.rsqrt(jnp.mean(x * x, axis=-1, keepdims=True) + RMS_EPS)
    return (x * r) * mod_ref[row + 1:row + 2, :] + mod_ref[row:row + 1, :]


def _sigmoid(x):
    return 1.0 / (1.0 + jnp.exp(-x))


def _log_sigmoid(x):
    return jnp.minimum(x, 0.0) - jnp.log1p(jnp.exp(-jnp.abs(x)))


def _ada_kernel(c_ref, w_ref, b_ref, mul_ref, add_ref, o_ref):
    c = c_ref[...]
    ca = (c * _sigmoid(c)).astype(BF16)
    mod = jnp.dot(ca, w_ref[...].astype(BF16), preferred_element_type=F32) + b_ref[...]
    o_ref[...] = mul_ref[...] * mod + add_ref[...]


def _ada(c, ada_w, ada_b, mul, add):
    depth, d, _ = ada_w.shape
    b = c.shape[0]
    row = pl.BlockSpec((None, None, 1, d), lambda i, j: (i, j, 0, 0))
    out = pl.pallas_call(
        _ada_kernel,
        grid=(depth, 6),
        in_specs=[pl.BlockSpec((b, d), lambda i, j: (0, 0)),
                  pl.BlockSpec((None, d, d), lambda i, j: (i, 0, j)),
                  row, row, row],
        out_specs=pl.BlockSpec((None, None, b, d), lambda i, j: (i, j, 0, 0)),
        out_shape=jax.ShapeDtypeStruct((depth, 6, b, d), F32),
        compiler_params=_cparams(("arbitrary", "arbitrary")),
        name="ada_mod",
    )(c, ada_w, ada_b.reshape(depth, 6, 1, d), mul, add)
    return out.transpose(0, 2, 1, 3)


def _mlproj_kernel(x_ref, mod_ref, wq_ref, wkt_ref, wvo_ref, wg_ref, bg_ref,
                   q_ref, kt_ref, vo_ref, g_ref):
    h = _norm_mod(x_ref[...], mod_ref, 0).astype(BF16)
    q = jnp.dot(h, wq_ref[...], preferred_element_type=F32)
    q_ref[...] = (q * (ML_DQK ** -0.5)).astype(BF16)
    kt = lax.dot_general(wkt_ref[...], h, (((1,), (1,)), ((), ())), preferred_element_type=F32)
    kt_ref[...] = kt.astype(BF16)
    vo_ref[...] = jnp.dot(h, wvo_ref[...], preferred_element_type=F32).astype(BF16)
    g_ref[...] = jnp.dot(h, wg_ref[...], preferred_element_type=F32) + bg_ref[...]


def _mlproj(x2, mod, wq, wkt, wvo, wg, bg, seq):
    m, d = x2.shape
    tm = min(PROJ_TM, seq)
    tpb = seq // tm
    nq, nvo = wq.shape[1], wvo.shape[1]
    full = lambda shape: pl.BlockSpec(shape, lambda i: (0,) * len(shape))
    return pl.pallas_call(
        _mlproj_kernel,
        grid=(m // tm,),
        in_specs=[pl.BlockSpec((tm, d), lambda i: (i, 0)),
                  pl.BlockSpec((None, 6, d), lambda i: (i // tpb, 0, 0)),
                  full(wq.shape), full(wkt.shape), full(wvo.shape), full(wg.shape), full(bg.shape)],
        out_specs=[pl.BlockSpec((tm, nq), lambda i: (i, 0)),
                   pl.BlockSpec((nq, tm), lambda i: (0, i)),
                   pl.BlockSpec((tm, nvo), lambda i: (i, 0)),
                   pl.BlockSpec((tm, LANES), lambda i: (i, 0))],
        out_shape=[jax.ShapeDtypeStruct((m, nq), BF16),
                   jax.ShapeDtypeStruct((nq, m), BF16),
                   jax.ShapeDtypeStruct((m, nvo), BF16),
                   jax.ShapeDtypeStruct((m, LANES), F32)],
        compiler_params=_cparams(("parallel",)),
        name="mlstm_proj",
    )(x2, mod, wq, wkt, wvo, wg, bg)


def _gateprep_kernel(g_ref, o_ref, *, chunk):
    n = chunk
    r = lax.broadcasted_iota(jnp.int32, (n, n), 0)
    s = lax.broadcasted_iota(jnp.int32, (n, n), 1)
    tril = jnp.where(s <= r, 1.0, 0.0).astype(F32)
    lane = lax.broadcasted_iota(jnp.int32, (n, LANES), 1)
    for c in range(g_ref.shape[0] // chunk):
        g = g_ref[c * chunk:(c + 1) * chunk, :]
        b = jnp.dot(tril, _log_sigmoid(g), precision=HIGHEST, preferred_element_type=F32)
        b_i = pltpu.roll(b, LANES - ML_HEADS, axis=1)
        o_ref[c * chunk:(c + 1) * chunk, :] = jnp.where(lane < ML_HEADS, g - b_i, b)


def _gateprep(gates, chunk, seq):
    m = gates.shape[0]
    rows = min(GATEPREP_ROWS, seq)
    return pl.pallas_call(
        functools.partial(_gateprep_kernel, chunk=chunk),
        grid=(m // rows,),
        in_specs=[pl.BlockSpec((rows, LANES), lambda i: (i, 0))],
        out_specs=pl.BlockSpec((rows, LANES), lambda i: (i, 0)),
        out_shape=jax.ShapeDtypeStruct((m, LANES), F32),
        compiler_params=_cparams(("parallel",)),
        name="mlstm_gateprep",
    )(gates)


def _mlstm_kernel(x_ref, mod_ref, q_ref, kt_ref, vo_ref, arow_ref, col_ref, hg_ref, wout_ref,
                  o_ref, c_scr, m_scr, hs_scr):
    L = q_ref.shape[0]
    vw = ML_HEADS * ML_DV

    @pl.when(pl.program_id(1) == 0)
    def _():
        c_scr[...] = jnp.zeros_like(c_scr)
        m_scr[...] = jnp.zeros_like(m_scr)

    t_idx = lax.broadcasted_iota(jnp.int32, (L, L), 0)
    s_idx = lax.broadcasted_iota(jnp.int32, (L, L), 1)
    causal = s_idx <= t_idx
    lane_q = lax.broadcasted_iota(jnp.int32, (1, 2 * ML_DQK), 1)
    head_mask = [jnp.where(lane_q < ML_DQK, 1.0, 0.0).astype(BF16),
                 jnp.where(lane_q >= ML_DQK, 1.0, 0.0).astype(BF16)]
    lane_v = lax.broadcasted_iota(jnp.int32, (L, ML_DV), 1)
    ones_blk = jnp.ones((L, ML_DV), BF16)

    heads = range(ML_HEADS)
    m_all = [m_scr[h] for h in heads]
    c_all = [c_scr[h] for h in heads]
    a_rows = [arow_ref[h:h + 1, :] for h in heads]
    b_cols = [col_ref[:, ML_HEADS + h:ML_HEADS + h + 1] for h in heads]

    a_mats = [jnp.where(causal, a_rows[h], -jnp.inf) for h in heads]
    m_cols = [jnp.maximum(m_all[h], jnp.max(a_mats[h], axis=1, keepdims=True)) for h in heads]
    ws = [jnp.exp(a_mats[h] - m_cols[h]) for h in heads]
    e_cols = [jnp.exp(m_all[h] - m_cols[h]) for h in heads]
    floors = [jnp.exp(-(b_cols[h] + m_cols[h])) for h in heads]

    v_augs, ss, inters = [], [], []
    for h in heads:
        p, half = h // 2, h % 2
        qm = q_ref[:, p * 128:(p + 1) * 128] * head_mask[half]
        kt2 = kt_ref[p * 128:(p + 1) * 128, :]
        v_augs.append(jnp.concatenate([vo_ref[:, h * ML_DV:(h + 1) * ML_DV], ones_blk], axis=1))
        ss.append((jnp.dot(qm, kt2, preferred_element_type=F32) * ws[h]).astype(BF16))
        c_pair = jnp.concatenate([c_all[h], c_all[h]], axis=0).astype(BF16)
        inters.append(jnp.dot(qm, c_pair, preferred_element_type=F32))

    hhs = []
    for h in heads:
        num_aug = jnp.dot(ss[h], v_augs[h], preferred_element_type=F32) + e_cols[h] * inters[h]
        hhs.append(num_aug[:, :ML_DV] / jnp.maximum(jnp.abs(num_aug[:, ML_DV:2 * ML_DV]), floors[h]))

    rs = [lax.rsqrt(jnp.mean(hhs[h] * hhs[h], axis=-1, keepdims=True) + RMS_EPS) for h in heads]
    hs_all = []
    for h in heads:
        og = vo_ref[:, vw + h * ML_DV: vw + (h + 1) * ML_DV]
        hn = hhs[h] * rs[h] * hg_ref[:, h * ML_DV:(h + 1) * ML_DV] * _sigmoid(og.astype(F32))
        hs_all.append(hn.astype(BF16))

    c_new, m_new = [], []
    for h in heads:
        m_last = m_cols[h][L - 1:L, :]
        w_s = jnp.exp(a_rows[h] - m_last)
        decay = jnp.exp(m_all[h] - m_last)
        kw = (kt_ref[h * ML_DQK:(h + 1) * ML_DQK, :].astype(F32) * w_s).astype(BF16)
        c_new.append(decay * c_all[h] + jnp.dot(kw, v_augs[h], preferred_element_type=F32))
        m_new.append(b_cols[h][L - 1:L, :] + m_last)

    for h in range(ML_HEADS):
        c_scr[h] = c_new[h]
        m_scr[h] = m_new[h]
    mix = jnp.dot(jnp.concatenate(hs_all, axis=1), wout_ref[...], preferred_element_type=F32)
    o_ref[...] = x_ref[...] + mod_ref[2:3, :] * mix


def _mlstm(x2, mod, q, kt, vo, a_rows, cols, head_g, w_out, batch, seq):
    m, d = x2.shape
    L = min(ML_CHUNK, seq)
    nc = seq // L
    nq, nvo = q.shape[1], vo.shape[1]
    row_blk = lambda w: pl.BlockSpec((L, w), lambda b, c: (b * nc + c, 0))
    return pl.pallas_call(
        _mlstm_kernel,
        grid=(batch, nc),
        in_specs=[row_blk(d),
                  pl.BlockSpec((None, 6, d), lambda b, c: (b, 0, 0)),
                  row_blk(nq),
                  pl.BlockSpec((nq, L), lambda b, c: (0, b * nc + c)),
                  row_blk(nvo),
                  pl.BlockSpec((ML_HEADS, L), lambda b, c: (0, b * nc + c)),
                  row_blk(LANES),
                  pl.BlockSpec((1, d), lambda b, c: (0, 0)),
                  pl.BlockSpec((d, d), lambda b, c: (0, 0))],
        out_specs=row_blk(d),
        out_shape=jax.ShapeDtypeStruct((m, d), F32),
        scratch_shapes=[pltpu.VMEM((ML_HEADS, ML_DQK, 2 * ML_DV), F32),
                        pltpu.VMEM((ML_HEADS, 1, 1), F32),
                        pltpu.VMEM((L, d), BF16)],
        compiler_params=_cparams(("parallel", "arbitrary")),
        name="mlstm_core",
    )(x2, mod, q, kt, vo, a_rows, cols, head_g, w_out)


def _lru_kernel(x_ref, mod_ref, gx_ref, cw_ref, cb_ref, wgx_ref, wga_ref, bgx_ref, bga_ref, ap_ref,
                wout_ref, o_ref, xbuf, a_scr, u_scr, h_scr, hc_scr):
    T, W = h_scr.shape
    PAD = SUBLANES

    @pl.when(pl.program_id(1) == 0)
    def _():
        xbuf[0:PAD, :] = jnp.zeros((PAD, W), F32)
        hc_scr[...] = jnp.zeros_like(hc_scr)

    xbuf[PAD:PAD + T, :] = gx_ref[:, W:2 * W].astype(F32)
    xc = cb_ref[...] + xbuf[PAD - 3:PAD - 3 + T, :] * cw_ref[0:1, :]
    for j in range(1, CONV_W):
        xc = xc + xbuf[PAD - 3 + j:PAD - 3 + j + T, :] * cw_ref[j:j + 1, :]
    xbuf[0:PAD, :] = xbuf[T:T + PAD, :]

    xcb = xc.astype(BF16)
    sp = ap_ref[...]
    sp = jnp.maximum(-sp, 0.0) + jnp.log1p(jnp.exp(-jnp.abs(sp)))
    for hd in range(LRU_HEADS):
        sl = slice(hd * LRU_BW, (hd + 1) * LRU_BW)
        xh = xcb[:, sl]
        gxh = _sigmoid(jnp.dot(xh, wgx_ref[hd], preferred_element_type=F32) + bgx_ref[:, sl])
        gah = _sigmoid(jnp.dot(xh, wga_ref[hd], preferred_element_type=F32) + bga_ref[:, sl])
        log_a = (-LRU_C) * gah * sp[:, sl]
        a = jnp.exp(log_a)
        a_scr[:, sl] = a
        u_scr[:, sl] = xc[:, sl] * gxh * jnp.sqrt(jnp.tanh(-log_a) * (1.0 + a * a))

    row = lax.broadcasted_iota(jnp.int32, (SUBLANES, W), 0)

    first = row == 0

    def body(r, hc):
        base = pl.multiple_of(r * SUBLANES, SUBLANES)
        a8 = a_scr[pl.ds(base, SUBLANES), :]
        u8 = u_scr[pl.ds(base, SUBLANES), :]
        u8 = u8 + jnp.where(first, a8 * hc, 0.0)
        a8 = jnp.where(first, 0.0, a8)
        for sft in (1, 2, 4):
            u8 = a8 * pltpu.roll(u8, sft, axis=0) + u8
            if sft < 4:
                a8 = a8 * pltpu.roll(a8, sft, axis=0)
        h_scr[pl.ds(base, SUBLANES), :] = u8
        return u8[SUBLANES - 1:SUBLANES, :]

    hc_scr[...] = lax.fori_loop(0, T // SUBLANES, body, hc_scr[...], unroll=2)

    gb = gx_ref[:, 0:W].astype(F32)
    y = (h_scr[...] * jax.nn.gelu(gb)).astype(BF16)
    mix = jnp.dot(y, wout_ref[...], preferred_element_type=F32)
    o_ref[...] = x_ref[...] + mod_ref[2:3, :] * mix


def _lru(x2, mod, gx, conv_w, conv_b, wgx, wga, bgx, bga, a_param, w_out, batch, seq):
    m, d = x2.shape
    w = w_out.shape[0]
    T = min(LRU_TILE, seq)
    nt = seq // T
    row_blk = lambda wd: pl.BlockSpec((T, wd), lambda b, t: (b * nt + t, 0))
    full = lambda a: pl.BlockSpec(a.shape, lambda b, t: (0,) * a.ndim)
    return pl.pallas_call(
        _lru_kernel,
        grid=(batch, nt),
        in_specs=[row_blk(d),
                  pl.BlockSpec((None, 6, d), lambda b, t: (b, 0, 0)),
                  row_blk(2 * w),
                  full(conv_w), full(conv_b), full(wgx), full(wga), full(bgx), full(bga), full(a_param),
                  full(w_out)],
        out_specs=row_blk(d),
        out_shape=jax.ShapeDtypeStruct((m, d), F32),
        scratch_shapes=[pltpu.VMEM((T + SUBLANES, w), F32),
                        pltpu.VMEM((T, w), F32),
                        pltpu.VMEM((T, w), F32),
                        pltpu.VMEM((T, w), F32),
                        pltpu.VMEM((1, w), F32)],
        compiler_params=_cparams(("parallel", "arbitrary")),
        name="rglru_core",
    )(x2, mod, gx, conv_w, conv_b, wgx, wga, bgx, bga, a_param, w_out)


def _lruproj_kernel(x_ref, mod_ref, w_ref, o_ref):
    h = _norm_mod(x_ref[...], mod_ref, 0).astype(BF16)
    o_ref[...] = jnp.dot(h, w_ref[...], preferred_element_type=F32).astype(BF16)


def _lruproj(x2, mod, w, seq):
    m, d = x2.shape
    n = w.shape[1]
    tm = min(PROJ_TM, seq)
    tpb = seq // tm
    return pl.pallas_call(
        _lruproj_kernel,
        grid=(m // tm,),
        in_specs=[pl.BlockSpec((tm, d), lambda i: (i, 0)),
                  pl.BlockSpec((None, 6, d), lambda i: (i // tpb, 0, 0)),
                  pl.BlockSpec((d, n), lambda i: (0, 0))],
        out_specs=pl.BlockSpec((tm, n), lambda i: (i, 0)),
        out_shape=jax.ShapeDtypeStruct((m, n), BF16),
        compiler_params=_cparams(("parallel",)),
        name="rglru_proj",
    )(x2, mod, w)


def _final_norm(x, g_ref):
    return x * lax.rsqrt(jnp.mean(x * x, axis=-1, keepdims=True) + RMS_EPS) * g_ref[...]


def _swiglu(h, w13_ref, w2_ref):
    dff = w2_ref.shape[0]
    acc = None
    for c in range(dff // FFN_TF):
        lo, hi = c * FFN_TF, (c + 1) * FFN_TF
        a = jnp.dot(h, w13_ref[:, lo:hi], preferred_element_type=F32)
        b = jnp.dot(h, w13_ref[:, dff + lo:dff + hi], preferred_element_type=F32)
        g = (a * _sigmoid(a) * b).astype(BF16)
        y = jnp.dot(g, w2_ref[lo:hi, :], preferred_element_type=F32)
        acc = y if acc is None else acc + y
    return acc


def _ffn_kernel(x_ref, mod_ref, w13_ref, w2_ref, o_ref):
    x = x_ref[...]
    h = _norm_mod(x, mod_ref, 3).astype(BF16)
    o_ref[...] = x + mod_ref[5:6, :] * _swiglu(h, w13_ref, w2_ref)


def _ffn(x2, mod, w13, w2, layer, seq):
    m, d = x2.shape
    dff = w2.shape[1]
    tm = min(FFN_TM, seq)
    tpb = seq // tm
    return pl.pallas_call(
        _ffn_kernel,
        grid=(m // tm,),
        in_specs=[pl.BlockSpec((tm, d), lambda i: (i, 0)),
                  pl.BlockSpec((None, 6, d), lambda i: (i // tpb, 0, 0)),
                  pl.BlockSpec((None, d, 2 * dff), lambda i: (layer, 0, 0), pipeline_mode=pl.Buffered(1)),
                  pl.BlockSpec((None, dff, d), lambda i: (layer, 0, 0), pipeline_mode=pl.Buffered(1))],
        out_specs=pl.BlockSpec((tm, d), lambda i: (i, 0)),
        out_shape=jax.ShapeDtypeStruct((m, d), F32),
        compiler_params=_cparams(("parallel",)),
        name="ffn_swiglu",
    )(x2, mod, w13, w2)


def _router_kernel(x_ref, mod_ref, rw_ref, rwh_ref, rb_ref, info_ref, info_t_ref):
    h = _norm_mod(x_ref[...], mod_ref, 3)
    h_hi = h.astype(BF16)
    h_lo = (h - h_hi.astype(F32)).astype(BF16)
    p_hi = jnp.dot(h_hi, rw_ref[...], preferred_element_type=F32)
    p_lo = jnp.dot(h_lo, rwh_ref[...], preferred_element_type=F32)
    logits = p_hi + pltpu.roll(p_hi, LANES - N_EXPERTS, axis=1) + p_lo + rb_ref[...]
    lt = logits.T[0:N_EXPERTS, :]
    tm = lt.shape[1]
    sub = lax.broadcasted_iota(jnp.int32, lt.shape, 0).astype(F32)
    v1 = jnp.max(lt, axis=0, keepdims=True)
    i1 = jnp.min(jnp.where(lt == v1, sub, float(N_EXPERTS)), axis=0, keepdims=True)
    rest = jnp.where(sub == i1, -jnp.inf, lt)
    v2 = jnp.max(rest, axis=0, keepdims=True)
    i2 = jnp.min(jnp.where(rest == v2, sub, float(N_EXPERTS)), axis=0, keepdims=True)
    e2 = jnp.exp(v2 - v1)
    w1 = 1.0 / (1.0 + e2)
    w2 = e2 / (1.0 + e2)
    info_t = jnp.where(sub == 0.0, w1, jnp.where(sub == 1.0, w2, jnp.where(sub == 2.0, i1,
                                                                          jnp.where(sub == 3.0, i2, 0.0))))
    info_t_ref[...] = info_t
    info_ref[...] = jnp.concatenate([info_t, jnp.zeros((LANES - N_EXPERTS, tm), F32)], axis=0).T


def _router(x2, mod, router_w, router_b, seq):
    m, d = x2.shape
    tm = min(ROUTER_TM, seq)
    tpb = seq // tm
    ne = router_w.shape[1]
    w_hi = router_w.astype(BF16)
    w_lo = (router_w - w_hi.astype(F32)).astype(BF16)
    rw = jnp.pad(jnp.concatenate([w_hi, w_lo], axis=1), ((0, 0), (0, LANES - 2 * ne)))
    rwh = jnp.pad(w_hi, ((0, 0), (0, LANES - ne)))
    rb = jnp.pad(router_b, (0, LANES - ne), constant_values=-jnp.inf).reshape(1, LANES)
    return pl.pallas_call(
        _router_kernel,
        grid=(m // tm,),
        in_specs=[pl.BlockSpec((tm, d), lambda i: (i, 0)),
                  pl.BlockSpec((None, 6, d), lambda i: (i // tpb, 0, 0)),
                  pl.BlockSpec((d, LANES), lambda i: (0, 0)),
                  pl.BlockSpec((d, LANES), lambda i: (0, 0)),
                  pl.BlockSpec((1, LANES), lambda i: (0, 0))],
        out_specs=[pl.BlockSpec((tm, LANES), lambda i: (i, 0)),
                   pl.BlockSpec((SUBLANES, tm), lambda i: (0, i))],
        out_shape=[jax.ShapeDtypeStruct((m, LANES), F32),
                   jax.ShapeDtypeStruct((SUBLANES, m), F32)],
        compiler_params=_cparams(("parallel",)),
        name="moe_router",
    )(x2, mod, rw, rwh, rb)


def _moe_meta_kernel(sel_ref, tri_ref, pos_ref, meta_ref, run1, run2, off1, off2, *, tm):
    p = pl.program_id(0)
    i = pl.program_id(1)
    T = tri_ref.shape[0]

    @pl.when((p == 0) & (i == 0))
    def _():
        run1[...] = jnp.zeros_like(run1)
        run2[...] = jnp.zeros_like(run2)

    @pl.when((p == 1) & (i == 0))
    def _():
        cnt1 = run1[...]
        tot = cnt1 + run2[...]
        padded = jnp.floor((tot + (tm - 1)) * (1.0 / tm)) * tm
        e_idx = lax.broadcasted_iota(jnp.int32, padded.shape, 0)
        off = jnp.zeros_like(padded)
        for e in range(N_EXPERTS - 1):
            off = off + jnp.where(e_idx > e, padded[e:e + 1, :], 0.0)
        off1[...] = off
        off2[...] = off + cnt1
        meta_ref[0:SUBLANES, :] = off
        meta_ref[SUBLANES:2 * SUBLANES, :] = padded
        meta_ref[2 * SUBLANES:3 * SUBLANES, :] = tot
        run1[...] = jnp.zeros_like(run1)
        run2[...] = jnp.zeros_like(run2)

    row = lax.broadcasted_iota(jnp.int32, (SUBLANES, T), 0)
    sub = row.astype(F32)
    for sb in range(sel_ref.shape[1] // T):
        cols = slice(sb * T, (sb + 1) * T)
        oh1 = jnp.where(sel_ref[2:3, cols] == sub, 1.0, 0.0)
        oh2 = jnp.where(sel_ref[3:4, cols] == sub, 1.0, 0.0)

        @pl.when(p == 1)
        def _():
            tri = tri_ref[...]
            pre1 = jnp.dot(oh1.astype(BF16), tri, preferred_element_type=F32)
            pre2 = jnp.dot(oh2.astype(BF16), tri, preferred_element_type=F32)
            pos1 = jnp.sum(oh1 * (pre1 + (off1[:, 0:1] + run1[:, 0:1])), axis=0, keepdims=True)
            pos2 = jnp.sum(oh2 * (pre2 + (off2[:, 0:1] + run2[:, 0:1])), axis=0, keepdims=True)
            pos = jnp.where(row == 0, pos1, jnp.where(row == 1, pos2, 0.0))
            pos_ref[:, cols] = pos.astype(jnp.int32)

        run1[...] += jnp.sum(oh1, axis=1, keepdims=True)
        run2[...] += jnp.sum(oh2, axis=1, keepdims=True)


def _moe_meta(info_t, tm):
    m = info_t.shape[1]
    T = min(MOE_META_T, m)
    W = min(MOE_META_W, m)
    s_i = lax.broadcasted_iota(jnp.int32, (T, T), 0)
    t_i = lax.broadcasted_iota(jnp.int32, (T, T), 1)
    tri = (s_i < t_i).astype(BF16)
    return pl.pallas_call(
        functools.partial(_moe_meta_kernel, tm=tm),
        grid=(2, m // W),
        in_specs=[pl.BlockSpec((SUBLANES, W), lambda p, i: (0, i)),
                  pl.BlockSpec((T, T), lambda p, i: (0, 0))],
        out_specs=[pl.BlockSpec((SUBLANES, W), lambda p, i: (0, i * p)),
                   pl.BlockSpec((3 * SUBLANES, LANES), lambda p, i: (0, 0))],
        out_shape=[jax.ShapeDtypeStruct((SUBLANES, m), jnp.int32),
                   jax.ShapeDtypeStruct((3 * SUBLANES, LANES), F32)],
        scratch_shapes=[pltpu.VMEM((SUBLANES, LANES), F32)] * 4,
        compiler_params=_cparams(("arbitrary", "arbitrary")),
        name="moe_meta",
    )(info_t, tri)


def _row_copy(src_ref, src_row, dst_ref, dst_row, sem):
    return pltpu.make_async_copy(src_ref.at[pl.ds(src_row, 1), :], dst_ref.at[pl.ds(dst_row, 1), :], sem)


def _moe_dispatch_kernel(fill_ref, pos1_ref, pos2_ref, x_ref, mod_ref, hg_ref, h_scr, z_scr, sem, zsem, *, tm):
    T = x_ref.shape[0]
    ne = N_EXPERTS

    def zero_row(e, r):
        return _row_copy(z_scr, 0, hg_ref, fill_ref[e] + r, zsem)

    def zero_tile(t):
        row0 = pl.multiple_of((fill_ref[2 * ne] + t) * tm, tm)
        return pltpu.make_async_copy(z_scr, hg_ref.at[pl.ds(row0, tm), :], zsem)

    def for_each_fill(fn):
        for e in range(ne):
            lax.fori_loop(0, fill_ref[ne + e], lambda r, c, e=e: (fn(zero_row(e, r)), c)[1], 0)
        lax.fori_loop(0, fill_ref[2 * ne + 1], lambda t, c: (fn(zero_tile(t)), c)[1], 0)

    @pl.when(pl.program_id(0) == 0)
    def _():
        z_scr[...] = jnp.zeros_like(z_scr)
        for_each_fill(lambda cp: cp.start())
        for_each_fill(lambda cp: cp.wait())

    h_scr[...] = _norm_mod(x_ref[...], mod_ref, 3)

    def body(j, carry):
        _row_copy(h_scr, j, hg_ref, pos1_ref[j], sem).start()
        _row_copy(h_scr, j, hg_ref, pos2_ref[j], sem).start()
        return carry

    lax.fori_loop(0, T, body, 0, unroll=32)
    pltpu.make_async_copy(h_scr, hg_ref.at[pl.ds(0, T), :], sem).wait()
    pltpu.make_async_copy(h_scr, hg_ref.at[pl.ds(0, T), :], sem).wait()


def _moe_dispatch(fill, pos1, pos2, x2, mod, n_rows, tm, seq):
    m, d = x2.shape
    T = min(MOE_DISPATCH_T, seq)
    tpb = seq // T
    smem = lambda: pl.BlockSpec((T,), lambda i, fill: (i,), memory_space=pltpu.SMEM)
    grid_spec = pltpu.PrefetchScalarGridSpec(
        num_scalar_prefetch=1,
        grid=(m // T,),
        in_specs=[smem(), smem(),
                  pl.BlockSpec((T, d), lambda i, fill: (i, 0)),
                  pl.BlockSpec((None, 6, d), lambda i, fill: (i // tpb, 0, 0))],
        out_specs=pl.BlockSpec(memory_space=pl.ANY),
        scratch_shapes=[pltpu.VMEM((T, d), F32), pltpu.VMEM((tm, d), F32),
                        pltpu.SemaphoreType.DMA(()), pltpu.SemaphoreType.DMA(())])
    return pl.pallas_call(
        functools.partial(_moe_dispatch_kernel, tm=tm),
        grid_spec=grid_spec,
        out_shape=jax.ShapeDtypeStruct((n_rows, d), F32),
        compiler_params=_cparams(("arbitrary",)),
        name="moe_dispatch",
    )(fill, pos1, pos2, x2, mod)


def _moe_group_kernel(eid_ref, nv_ref, hg_ref, w13_ref, w2_ref, o_ref):
    del eid_ref
    r = pl.program_id(0)

    @pl.when(r < nv_ref[0])
    def _():
        o_ref[...] = _swiglu(hg_ref[...].astype(BF16), w13_ref, w2_ref)

    @pl.when(r >= nv_ref[0])
    def _():
        o_ref[...] = jnp.zeros_like(o_ref)


def _moe_group(eid, nvalid, hg, w13, w2, layer, tm):
    n_rows, d = hg.shape
    dff = w2.shape[2]
    grid_spec = pltpu.PrefetchScalarGridSpec(
        num_scalar_prefetch=2,
        grid=(n_rows // tm,),
        in_specs=[pl.BlockSpec((tm, d), lambda r, eid, nv: (jnp.minimum(r, nv[0] - 1), 0)),
                  pl.BlockSpec((None, None, d, 2 * dff), lambda r, eid, nv: (layer, eid[r], 0, 0),
                               pipeline_mode=pl.Buffered(1)),
                  pl.BlockSpec((None, None, dff, d), lambda r, eid, nv: (layer, eid[r], 0, 0),
                               pipeline_mode=pl.Buffered(1))],
        out_specs=pl.BlockSpec((tm, d), lambda r, eid, nv: (r, 0)))
    return pl.pallas_call(
        _moe_group_kernel,
        grid_spec=grid_spec,
        out_shape=jax.ShapeDtypeStruct((n_rows, d), F32),
        compiler_params=_cparams(("arbitrary",)),
        name="moe_experts",
    )(eid, nvalid, hg, w13, w2)


def _moe_combine_kernel(pos1_ref, pos2_ref, x_ref, mod_ref, info_ref, fg_ref, ys_ref, o_ref, y1, y2, sem,
                        *, final):
    T = x_ref.shape[0]

    def body(j, carry):
        _row_copy(ys_ref, pos1_ref[j], y1, j, sem).start()
        _row_copy(ys_ref, pos2_ref[j], y2, j, sem).start()
        return carry

    lax.fori_loop(0, T, body, 0, unroll=32)
    pltpu.make_async_copy(ys_ref.at[pl.ds(0, T), :], y1, sem).wait()
    pltpu.make_async_copy(ys_ref.at[pl.ds(0, T), :], y2, sem).wait()
    info = info_ref[...]
    y = info[:, 0:1] * y1[...] + info[:, 1:2] * y2[...]
    out = x_ref[...] + mod_ref[5:6, :] * y
    o_ref[...] = _final_norm(out, fg_ref) if final else out


def _moe_combine(pos1, pos2, x2, mod, info, final_g, ys, final, seq):
    m, d = x2.shape
    T = min(MOE_DMA_T, seq)
    tpb = seq // T
    smem = lambda: pl.BlockSpec((T,), lambda i: (i,), memory_space=pltpu.SMEM)
    return pl.pallas_call(
        functools.partial(_moe_combine_kernel, final=final),
        grid=(m // T,),
        in_specs=[smem(), smem(),
                  pl.BlockSpec((T, d), lambda i: (i, 0)),
                  pl.BlockSpec((None, 6, d), lambda i: (i // tpb, 0, 0)),
                  pl.BlockSpec((T, LANES), lambda i: (i, 0)),
                  pl.BlockSpec((1, d), lambda i: (0, 0)),
                  pl.BlockSpec(memory_space=pl.ANY)],
        out_specs=pl.BlockSpec((T, d), lambda i: (i, 0)),
        out_shape=jax.ShapeDtypeStruct((m, d), F32),
        scratch_shapes=[pltpu.VMEM((T, d), F32), pltpu.VMEM((T, d), F32), pltpu.SemaphoreType.DMA(())],
        compiler_params=_cparams(("arbitrary",)),
        name="moe_combine",
    )(pos1, pos2, x2, mod, info, final_g, ys)


def _moe(x2, mod, rw, rb, w13, w2, layer, final_g, final, seq):
    m, d = x2.shape
    ne = w2.shape[1]
    tm = min(MOE_TM, seq)
    n_rows = 2 * m + ne * tm
    info, info_t = _router(x2, mod, rw, rb, seq)
    pos, meta = _moe_meta(info_t, tm)
    meta = meta[:, 0].astype(jnp.int32)
    off, padded, tot = (meta[k * SUBLANES:k * SUBLANES + ne] for k in range(3))
    ends = off + padded
    nvalid = ends[ne - 1] // tm
    starts = jnp.minimum(jnp.arange(n_rows // tm, dtype=jnp.int32), nvalid - 1) * tm
    eid = jnp.sum((starts[:, None] >= ends[None, :]).astype(jnp.int32), axis=1)
    fill = jnp.concatenate([off + tot, padded - tot, jnp.stack([nvalid, n_rows // tm - nvalid])])
    hg = _moe_dispatch(fill, pos[0], pos[1], x2, mod, n_rows, tm, seq)
    ys = _moe_group(eid, nvalid.reshape(1), hg, w13, w2, layer, tm)
    return _moe_combine(pos[0], pos[1], x2, mod, info, final_g, ys, final, seq)


def kernel(x, c, ada_w, ada_b, norm_g, ml_w_in, ml_gate_b, ml_head_g, ml_w_out, ffn_w13, ffn_w2,
           lru_w_in, lru_conv_w, lru_conv_b, lru_gate_w, lru_gate_b, lru_a_param, lru_w_out,
           moe_router_w, moe_router_b, moe_w13, moe_w2, final_norm_g):
    batch, seq, d = x.shape
    depth = ada_w.shape[0]
    m = batch * seq
    qkw = ML_HEADS * ML_DQK
    vw = ML_HEADS * ML_DV

    ones = jnp.ones((depth, 1, d), F32)
    zeros = jnp.zeros((depth, 1, d), F32)
    g0, g1 = norm_g[:, 0:1, :], norm_g[:, 1:2, :]
    mul = jnp.concatenate([ones, g0, ones, ones, g1, ones], axis=1).reshape(depth, 6, 1, d)
    add = jnp.concatenate([zeros, g0, zeros, zeros, g1, zeros], axis=1).reshape(depth, 6, 1, d)
    mods = _ada(c, ada_w, ada_b, mul, add)

    ffn_w13_b, ffn_w2_b = ffn_w13.astype(BF16), ffn_w2.astype(BF16)
    moe_w13_b, moe_w2_b = moe_w13.astype(BF16), moe_w2.astype(BF16)

    x2 = x.reshape(m, d)
    for i in range(depth):
        j = i // 2
        mod = mods[i]
        last = i == depth - 1
        if i % 2 == 0:
            w_in = ml_w_in[j]
            wq = w_in[:, :qkw].astype(BF16)
            wkt = w_in[:, qkw:2 * qkw].T.astype(BF16)
            wvo = w_in[:, 2 * qkw:2 * qkw + 2 * vw].astype(BF16)
            wg = jnp.pad(w_in[:, 2 * qkw + 2 * vw:], ((0, 0), (0, LANES - 2 * ML_HEADS))).astype(BF16)
            bg = jnp.pad(ml_gate_b[j], (0, LANES - 2 * ML_HEADS)).reshape(1, LANES)
            q, kt, vo, gates = _mlproj(x2, mod, wq, wkt, wvo, wg, bg, seq)
            cols = _gateprep(gates, min(ML_CHUNK, seq), seq)
            a_rows = cols[:, :ML_HEADS].T
            x2 = _mlstm(x2, mod, q, kt, vo, a_rows, cols, ml_head_g[j].reshape(1, vw),
                        ml_w_out[j].astype(BF16), batch, seq)
            x2 = _ffn(x2, mod, ffn_w13_b, ffn_w2_b, j, seq)
        else:
            gx = _lruproj(x2, mod, lru_w_in[j].astype(BF16), seq)
            gw = lru_gate_w[j]
            gb = lru_gate_b[j]
            wgx = gw[:, :, :LRU_BW].astype(BF16)
            wga = gw[:, :, LRU_BW:].astype(BF16)
            bgx = gb[:, :LRU_BW].reshape(1, -1)
            bga = gb[:, LRU_BW:].reshape(1, -1)
            x2 = _lru(x2, mod, gx, lru_conv_w[j], lru_conv_b[j].reshape(1, -1), wgx, wga, bgx, bga,
                      lru_a_param[j].reshape(1, -1), lru_w_out[j].astype(BF16), batch, seq)
            x2 = _moe(x2, mod, moe_router_w[j], moe_router_b[j], moe_w13_b, moe_w2_b, j, final_norm_g.reshape(1, d), last, seq)
    return x2.reshape(batch, seq, d)
```
